```python
import jax, jax.numpy as jnp
from jax import lax
import numpy as np

D_MODEL = 1024
BATCH = 2
SEQ = 8192
DEPTH = 1
DEC_BATCH = 128
DEC_SEQ = 4
PAST_LEN = 2048
PAGE_SIZE = 128

N_HEADS = 8
HEAD_DIM = 64
ATTN_WIDTH = N_HEADS * HEAD_DIM
MOBA_BLOCK = 256
MOBA_TOPK = 3
Q_BLOCK = 128
CONV_WIDTH = 3
CONV_CH = 512
D_FF = ((8 * D_MODEL + 3 * 256 - 1) // (3 * 256)) * 256
RMS_EPS = 1e-6
PROJ_SPLITS = (ATTN_WIDTH, ATTN_WIDTH, ATTN_WIDTH, CONV_CH, CONV_CH, CONV_CH, D_MODEL, D_MODEL)
PROJ_WIDTH = sum(PROJ_SPLITS)

kernel_name = "hybrid_moba_shortconv_decode_step"


def rmsnorm(x, g):
    xf = x.astype(jnp.float32)
    r = lax.rsqrt(jnp.mean(xf * xf, axis=-1, keepdims=True) + RMS_EPS)
    return (xf * r).astype(x.dtype) * g


def mixer_projections(x, g_pre, w_in):
    n, t, _ = x.shape
    xn = rmsnorm(x, g_pre)
    p = xn @ w_in
    cuts = tuple(int(c) for c in np.cumsum(PROJ_SPLITS)[:-1])
    q, k, v, cb, cc, cx, ga, gc = jnp.split(p, cuts, axis=-1)
    q = q.reshape(n, t, N_HEADS, HEAD_DIM)
    k = k.reshape(n, t, N_HEADS, HEAD_DIM)
    v = v.reshape(n, t, N_HEADS, HEAD_DIM)
    return q, k, v, cb, cc, cx, jax.nn.sigmoid(ga), jax.nn.sigmoid(gc)


def to_blocks(k, v):
    n, l, h, d = k.shape
    nb = -(-l // MOBA_BLOCK)
    pad = nb * MOBA_BLOCK - l
    padw = ((0, 0), (0, pad), (0, 0), (0, 0))
    kb = jnp.pad(k, padw).reshape(n, nb, MOBA_BLOCK, h, d).transpose(0, 3, 1, 2, 4)
    vb = jnp.pad(v, padw).reshape(n, nb, MOBA_BLOCK, h, d).transpose(0, 3, 1, 2, 4)
    kmean = jnp.mean(kb.astype(jnp.float32), axis=3).astype(k.dtype)
    return kb, vb, kmean


def moba_chunk(q_c, kb, vb, kmean, q_pos, b):
    n, h, nq, d = q_c.shape
    nb = kb.shape[2]
    k_sel = min(MOBA_TOPK, nb)
    scale = HEAD_DIM ** -0.5
    s_blk = jnp.einsum('nhqd,nhjd->nhqj', q_c, kmean).astype(jnp.float32)
    s_blk = jnp.where(jnp.arange(nb) < b, s_blk, -jnp.inf)
    _, idx = lax.top_k(s_blk, k_sel)
    valid = jnp.arange(k_sel) < b
    ni = jnp.arange(n)[:, None, None, None]
    hi = jnp.arange(h)[None, :, None, None]
    k_g = kb[ni, hi, idx]
    v_g = vb[ni, hi, idx]
    s_past = jnp.einsum('nhqd,nhqjsd->nhqjs', q_c, k_g).astype(jnp.float32) * scale
    s_past = jnp.where(valid[:, None], s_past, -jnp.inf).reshape(n, h, nq, k_sel * MOBA_BLOCK)
    k_own = lax.dynamic_index_in_dim(kb, b, axis=2, keepdims=False)
    v_own = lax.dynamic_index_in_dim(vb, b, axis=2, keepdims=False)
    s_own = jnp.einsum('nhqd,nhsd->nhqs', q_c, k_own).astype(jnp.float32) * scale
    key_pos = b * MOBA_BLOCK + jnp.arange(MOBA_BLOCK)
    s_own = jnp.where(key_pos[None, :] <= q_pos[:, None], s_own, -jnp.inf)
    p = jax.nn.softmax(jnp.concatenate([s_past, s_own], axis=-1), axis=-1).astype(v_own.dtype)
    p_past = p[..., :k_sel * MOBA_BLOCK].reshape(n, h, nq, k_sel, MOBA_BLOCK)
    p_own = p[..., k_sel * MOBA_BLOCK:]
    return (jnp.einsum('nhqjs,nhqjsd->nhqd', p_past, v_g)
            + jnp.einsum('nhqs,nhsd->nhqd', p_own, v_own))


def moba_prompt(q, k, v):
    bsz, s, h, d = q.shape
    kb, vb, kmean = to_blocks(k, v)
    n_chunks = s // Q_BLOCK
    qt = q.transpose(0, 2, 1, 3).reshape(bsz, h, n_chunks, Q_BLOCK, d).transpose(2, 0, 1, 3, 4)
    starts = jnp.arange(n_chunks, dtype=jnp.int32) * Q_BLOCK

    def body(args):
        q_c, start = args
        return moba_chunk(q_c, kb, vb, kmean, start + jnp.arange(Q_BLOCK, dtype=jnp.int32),
                          start // MOBA_BLOCK)

    o = lax.map(body, (qt, starts))
    return o.transpose(1, 0, 3, 2, 4).reshape(bsz, s, ATTN_WIDTH)


def moba_sample(q, k_new, v_new, cache_k, cache_v, page_table):
    n, t, h, d = q.shape
    past_len = page_table.shape[1] * PAGE_SIZE
    k_past = cache_k[page_table].reshape(n, past_len, h, d)
    v_past = cache_v[page_table].reshape(n, past_len, h, d)
    k_all = jnp.concatenate([k_past, k_new], axis=1)
    v_all = jnp.concatenate([v_past, v_new], axis=1)
    kb, vb, kmean = to_blocks(k_all, v_all)
    qt = q.transpose(1, 0, 2, 3)[:, :, :, None, :]
    pos = past_len + jnp.arange(t, dtype=jnp.int32)

    def body(args):
        q_c, p = args
        return moba_chunk(q_c, kb, vb, kmean, p[None], p // MOBA_BLOCK)

    o = lax.map(body, (qt, pos))
    return o[:, :, :, 0].transpose(1, 0, 2, 3).reshape(n, t, ATTN_WIDTH)


def short_conv(cb, cc, cx, prev, w_conv):
    u = cc * cx
    up = jnp.concatenate([prev, u], axis=1)
    t = u.shape[1]
    y = w_conv[0] * up[:, 0:t]
    for i in range(1, CONV_WIDTH):
        y = y + w_conv[i] * up[:, i:i + t]
    return cb * y, up[:, -(CONV_WIDTH - 1):]


def merge_and_ffn(x, attn_o, conv_o, ga, gc, w_proj_attn, w_proj_conv, w_out, g_attn_post,
                  g_ffn_pre, w_gate, w_up, w_down, g_ffn_post):
    m = ga * (attn_o @ w_proj_attn) + gc * (conv_o @ w_proj_conv)
    h = x + rmsnorm(m @ w_out, g_attn_post)
    hn = rmsnorm(h, g_ffn_pre)
    f = (jax.nn.silu(hn @ w_gate) * (hn @ w_up)) @ w_down
    return h + rmsnorm(f, g_ffn_post)


def setup_inputs(seed: int = 0) -> dict:
    key = jax.random.key(seed)
    ks = jax.random.split(key, 24)
    n_pages = PAST_LEN // PAGE_SIZE
    n_pool = (DEC_BATCH * n_pages * 5) // 4
    f32 = jnp.float32

    def w(k, shape, fan_in):
        return jax.random.normal(k, shape, f32) * fan_in ** -0.5

    def gain(k, shape):
        return 1.0 + 0.05 * jax.random.normal(k, shape, f32)

    perm = jax.random.permutation(ks[0], n_pool)
    page_table = perm[:DEC_BATCH * n_pages].reshape(DEC_BATCH, n_pages).astype(jnp.int32)
    return {
        "x_prompt": jax.random.normal(ks[1], (BATCH, SEQ, D_MODEL), f32),
        "x_sample": jax.random.normal(ks[2], (DEC_BATCH, DEC_SEQ, D_MODEL), f32),
        "cache_k": jax.random.normal(ks[3], (DEPTH, n_pool, PAGE_SIZE, N_HEADS, HEAD_DIM), f32),
        "cache_v": jax.random.normal(ks[4], (DEPTH, n_pool, PAGE_SIZE, N_HEADS, HEAD_DIM), f32),
        "state_conv": jax.random.normal(ks[5], (DEPTH, DEC_BATCH, CONV_WIDTH - 1, CONV_CH), f32),
        "page_table": page_table,
        "g_attn_pre": gain(ks[6], (DEPTH, D_MODEL)),
        "w_in": w(ks[7], (DEPTH, D_MODEL, PROJ_WIDTH), D_MODEL),
        "w_conv": w(ks[8], (DEPTH, CONV_WIDTH, CONV_CH), CONV_WIDTH),
        "w_proj_attn": w(ks[9], (DEPTH, ATTN_WIDTH, D_MODEL), ATTN_WIDTH),
        "w_proj_conv": w(ks[10], (DEPTH, CONV_CH, D_MODEL), CONV_CH),
        "w_out": w(ks[11], (DEPTH, D_MODEL, D_MODEL), D_MODEL),
        "g_attn_post": gain(ks[12], (DEPTH, D_MODEL)),
        "g_ffn_pre": gain(ks[13], (DEPTH, D_MODEL)),
        "w_gate": w(ks[14], (DEPTH, D_MODEL, D_FF), D_MODEL),
        "w_up": w(ks[15], (DEPTH, D_MODEL, D_FF), D_MODEL),
        "w_down": w(ks[16], (DEPTH, D_FF, D_MODEL), D_FF),
        "g_ffn_post": gain(ks[17], (DEPTH, D_MODEL)),
    }


def reference(x_prompt, x_sample, cache_k, cache_v, state_conv, page_table, g_attn_pre, w_in,
              w_conv, w_proj_attn, w_proj_conv, w_out, g_attn_post, g_ffn_pre, w_gate, w_up,
              w_down, g_ffn_post):
    xp, xs = x_prompt, x_sample
    kp_l, vp_l, cp_l, ks_l, vs_l, cs_l = [], [], [], [], [], []
    for l in range(DEPTH):
        tail = (w_proj_attn[l], w_proj_conv[l], w_out[l], g_attn_post[l], g_ffn_pre[l],
                w_gate[l], w_up[l], w_down[l], g_ffn_post[l])
        q, k, v, cb, cc, cx, ga, gc = mixer_projections(xp, g_attn_pre[l], w_in[l])
        attn_o = moba_prompt(q, k, v)
        prev0 = jnp.zeros((xp.shape[0], CONV_WIDTH - 1, CONV_CH), xp.dtype)
        conv_o, conv_p = short_conv(cb, cc, cx, prev0, w_conv[l])
        xp = merge_and_ffn(xp, attn_o, conv_o, ga, gc, *tail)
        kp_l.append(k); vp_l.append(v); cp_l.append(conv_p)
        q, k, v, cb, cc, cx, ga, gc = mixer_projections(xs, g_attn_pre[l], w_in[l])
        attn_o = moba_sample(q, k, v, cache_k[l], cache_v[l], page_table)
        conv_o, conv_s = short_conv(cb, cc, cx, state_conv[l], w_conv[l])
        xs = merge_and_ffn(xs, attn_o, conv_o, ga, gc, *tail)
        ks_l.append(k); vs_l.append(v); cs_l.append(conv_s)
    k_prompt = jnp.stack(kp_l)
    v_prompt = jnp.stack(vp_l)
    conv_prompt = jnp.stack(cp_l)
    k_sample = jnp.stack(ks_l)
    v_sample = jnp.stack(vs_l)
    conv_sample = jnp.stack(cs_l)
    return (xp, xs, k_prompt, v_prompt, conv_prompt, k_sample, v_sample, conv_sample)
```

```python
import functools

import jax
import jax.numpy as jnp
from jax import lax
from jax.experimental import pallas as pl
from jax.experimental.pallas import tpu as pltpu

N_HEADS = 8
HEAD_DIM = 64
ATTN_WIDTH = N_HEADS * HEAD_DIM
MOBA_BLOCK = 256
MOBA_TOPK = 3
CONV_WIDTH = 3
PAGE_SIZE = 128
RMS_EPS = 1e-6

LANES = 128
HEADS_PER_GROUP = LANES // HEAD_DIM
MASK_VALUE = -(2.0 ** 100)
VMEM_LIMIT = 52 * 1024 * 1024

F32 = jnp.float32
BF16 = jnp.bfloat16


def _dot(a, b):
    return jnp.dot(a, b, preferred_element_type=F32)


def _dot_nt(a, b):
    return lax.dot_general(a, b, (((1,), (1,)), ((), ())), preferred_element_type=F32)


def _rms(x):
    return x * lax.rsqrt(jnp.mean(x * x, axis=-1, keepdims=True) + RMS_EPS)


def _resident(shape):
    return pl.BlockSpec(shape, lambda *_: (0,) * len(shape), pipeline_mode=pl.Buffered(1))


def _proj_kernel(x_ref, g_ref, w_ref, wc_ref, b1_ref, b2_ref,
                 q_ref, k_ref, v_ref, co_ref, ga_ref, gc_ref, u_ref,
                 carry_ref, *, seq_rows, d_model):
    tm = x_ref.shape[0]
    cw = ATTN_WIDTH
    xn = (_rms(x_ref[...]) * g_ref[...]).astype(BF16)

    def proj(lo, width):
        return _dot(xn, w_ref[:, lo:lo + width])

    q_ref[...] = (proj(0, cw) * (HEAD_DIM ** -0.5)).astype(BF16)
    k_ref[...] = proj(cw, cw)
    v_ref[...] = proj(2 * cw, cw)
    cc = proj(4 * cw, cw)
    cx = proj(5 * cw, cw)
    u = cc * cx
    row = lax.broadcasted_iota(jnp.int32, u.shape, 0)
    s1 = pltpu.roll(u, 1, axis=0)
    s2 = pltpu.roll(u, 2, axis=0)
    if seq_rows is None:
        @pl.when(pl.program_id(1) == 0)
        def _():
            carry_ref[...] = b1_ref[0]
        c = carry_ref[...]
        prev1 = jnp.broadcast_to(c[7:8, :], u.shape)
        prev2 = jnp.broadcast_to(c[6:7, :], u.shape)
        s1 = jnp.where(row == 0, prev1, s1)
        s2 = jnp.where(row == 0, prev2, jnp.where(row == 1, prev1, s2))
        carry_ref[...] = u[tm - 8:tm, :]
        u_ref[0] = u[tm - 8:tm, :]
    else:
        t = row % seq_rows
        s1 = jnp.where(t >= 1, s1, b1_ref[...])
        s2 = jnp.where(t >= 2, s2, b2_ref[...])
        u_ref[...] = u
    wc = wc_ref[...]
    y = wc[0:1, :] * s2 + wc[1:2, :] * s1 + wc[2:3, :] * u
    co_ref[...] = (proj(3 * cw, cw) * y).astype(BF16)
    ga_ref[...] = jax.nn.sigmoid(proj(6 * cw, d_model))
    gc_ref[...] = jax.nn.sigmoid(proj(6 * cw + d_model, d_model))


def _proj_prompt(x, g, w_in, w_conv, prev, tm):
    bsz, s, d = x.shape
    cw = ATTN_WIDTH
    hist = jnp.zeros((bsz, 8, cw), F32).at[:, 8 - (CONV_WIDTH - 1):].set(prev)
    row_blk = lambda width: pl.BlockSpec((None, tm, width), lambda b, i: (b, i, 0))
    outs = pl.pallas_call(
        functools.partial(_proj_kernel, seq_rows=None, d_model=d),
        grid=(bsz, s // tm),
        in_specs=[
            row_blk(d),
            _resident((1, d)),
            _resident(w_in.shape),
            _resident(w_conv.shape),
            pl.BlockSpec((1, 8, cw), lambda b, i: (b, 0, 0)),
            pl.BlockSpec((1, 8, cw), lambda b, i: (b, 0, 0)),
        ],
        out_specs=[row_blk(cw), row_blk(cw), row_blk(cw), row_blk(cw), row_blk(d), row_blk(d),
                   pl.BlockSpec((1, 8, cw), lambda b, i: (b, 0, 0))],
        out_shape=[
            jax.ShapeDtypeStruct((bsz, s, cw), BF16),
            jax.ShapeDtypeStruct((bsz, s, cw), F32),
            jax.ShapeDtypeStruct((bsz, s, cw), F32),
            jax.ShapeDtypeStruct((bsz, s, cw), BF16),
            jax.ShapeDtypeStruct((bsz, s, d), F32),
            jax.ShapeDtypeStruct((bsz, s, d), F32),
            jax.ShapeDtypeStruct((bsz, 8, cw), F32),
        ],
        scratch_shapes=[pltpu.VMEM((8, cw), F32)],
        compiler_params=pltpu.CompilerParams(
            dimension_semantics=("arbitrary", "arbitrary"), vmem_limit_bytes=VMEM_LIMIT),
        name="proj_prompt",
    )(x, g.reshape(1, d), w_in, w_conv, hist, hist)
    return outs


def _proj_sample(x, g, w_in, w_conv, state):
    n, t, d = x.shape
    cw = ATTN_WIDTH
    rows = n * t
    b1 = jnp.zeros((n, t, cw), F32).at[:, 0].set(state[:, 1]).reshape(rows, cw)
    b2 = jnp.zeros((n, t, cw), F32).at[:, 0].set(state[:, 0]).at[:, 1].set(state[:, 1])
    b2 = b2.reshape(rows, cw)
    full = lambda width: pl.BlockSpec((rows, width), lambda i: (0, 0))
    outs = pl.pallas_call(
        functools.partial(_proj_kernel, seq_rows=t, d_model=d),
        grid=(1,),
        in_specs=[full(d), _resident((1, d)), _resident(w_in.shape), _resident(w_conv.shape),
                  full(cw), full(cw)],
        out_specs=[full(cw), full(cw), full(cw), full(cw), full(d), full(d), full(cw)],
        out_shape=[
            jax.ShapeDtypeStruct((rows, cw), BF16),
            jax.ShapeDtypeStruct((rows, cw), F32),
            jax.ShapeDtypeStruct((rows, cw), F32),
            jax.ShapeDtypeStruct((rows, cw), BF16),
            jax.ShapeDtypeStruct((rows, d), F32),
            jax.ShapeDtypeStruct((rows, d), F32),
            jax.ShapeDtypeStruct((rows, cw), F32),
        ],
        scratch_shapes=[pltpu.VMEM((8, cw), F32)],
        compiler_params=pltpu.CompilerParams(
            dimension_semantics=("arbitrary",), vmem_limit_bytes=VMEM_LIMIT),
        name="proj_sample",
    )(x.reshape(rows, d), g.reshape(1, d), w_in, w_conv, b1, b2)
    return outs


def _select_blocks(scores, n_valid, lane):
    cand = jnp.where(lane < n_valid, scores, -jnp.inf)
    sel = jnp.zeros(scores.shape, jnp.bool_)
    for _ in range(MOBA_TOPK):
        mx = jnp.max(cand, axis=-1, keepdims=True)
        idx = jnp.min(jnp.where(cand == mx, lane, LANES), axis=-1, keepdims=True)
        pick = (lane == idx) & (mx > -jnp.inf)
        sel = sel | pick
        cand = jnp.where(pick, -jnp.inf, cand)
    return sel


def _attn_prompt_kernel(q_ref, k_ref, v_ref, o_ref, kaug_ref, vb_ref, kmean_ref, *, n_blocks):
    t = pl.program_id(2)
    blk = MOBA_BLOCK

    @pl.when(t == 0)
    def _init():
        kmean_ref[...] = jnp.zeros(kmean_ref.shape, F32)
        lane_k = lax.broadcasted_iota(jnp.int32, (blk, LANES), 1)

        def fill(j, c):
            rows = pl.ds(pl.multiple_of(j * blk, blk), blk)
            kb = k_ref[rows, :]
            kaug_ref[rows, 0:LANES] = kb.astype(BF16)
            kaug_ref[rows, LANES:2 * LANES] = (lane_k == j).astype(BF16)
            vb_ref[rows, :] = v_ref[rows, :].astype(BF16)
            kmean_ref[pl.ds(j, 1), :] = jnp.sum(kb, axis=0, keepdims=True) * (1.0 / blk)
            return c

        lax.fori_loop(0, n_blocks, fill, 0)

    q = q_ref[...]
    lane = lax.broadcasted_iota(jnp.int32, (blk, LANES), 1)
    kmean = kmean_ref[...].astype(BF16)
    r2 = lax.broadcasted_iota(jnp.int32, (blk, blk), 0)
    c2 = lax.broadcasted_iota(jnp.int32, (blk, blk), 1)
    causal = c2 <= r2
    own = pl.ds(pl.multiple_of(t * blk, blk), blk)
    k_own = kaug_ref[own, 0:LANES]
    v_own = vb_ref[own, :]

    q_aug = []
    init = []
    for h in range(HEADS_PER_GROUP):
        in_head = (lane >= h * HEAD_DIM) & (lane < (h + 1) * HEAD_DIM)
        qh = jnp.where(in_head, q, jnp.zeros_like(q))
        sel = _select_blocks(_dot_nt(qh, kmean), t, lane)
        pen = jnp.where(sel, 0.0, MASK_VALUE).astype(BF16)
        q_aug.append(jnp.concatenate([qh, pen], axis=1))
        s = jnp.where(causal, _dot_nt(qh, k_own), MASK_VALUE)
        m = jnp.max(s, axis=-1, keepdims=True)
        p = jnp.exp(s - m)
        init.append((m, jnp.sum(p, axis=-1, keepdims=True), _dot(p.astype(BF16), v_own)))

    def past(j, carry):
        rows = pl.ds(pl.multiple_of(j * blk, blk), blk)
        kj = kaug_ref[rows, :]
        vj = vb_ref[rows, :]
        new = []
        for h in range(HEADS_PER_GROUP):
            m, l, acc = carry[h]
            s = _dot_nt(q_aug[h], kj)
            m_new = jnp.maximum(m, jnp.max(s, axis=-1, keepdims=True))
            a = jnp.exp(m - m_new)
            p = jnp.exp(s - m_new)
            new.append((m_new, a * l + jnp.sum(p, axis=-1, keepdims=True),
                        a * acc + _dot(p.astype(BF16), vj)))
        return tuple(new)

    fin = lax.fori_loop(0, t, past, tuple(init))
    out = [acc / l for (_, l, acc) in fin]
    o_ref[...] = jnp.where(lane < HEAD_DIM, out[0], out[1]).astype(o_ref.dtype)


def _attn_prompt(q, k, v):
    bsz, s, w = q.shape
    assert s % MOBA_BLOCK == 0 and s // MOBA_BLOCK <= LANES and HEADS_PER_GROUP == 2
    n_blocks = s // MOBA_BLOCK
    groups = w // LANES
    return pl.pallas_call(
        functools.partial(_attn_prompt_kernel, n_blocks=n_blocks),
        grid=(bsz, groups, n_blocks),
        in_specs=[
            pl.BlockSpec((None, MOBA_BLOCK, LANES), lambda b, g, t: (b, t, g)),
            pl.BlockSpec((None, s, LANES), lambda b, g, t: (b, 0, g)),
            pl.BlockSpec((None, s, LANES), lambda b, g, t: (b, 0, g)),
        ],
        out_specs=pl.BlockSpec((None, MOBA_BLOCK, LANES), lambda b, g, t: (b, t, g)),
        out_shape=jax.ShapeDtypeStruct((bsz, s, w), BF16),
        scratch_shapes=[
            pltpu.VMEM((s, 2 * LANES), BF16),
            pltpu.VMEM((s, LANES), BF16),
            pltpu.VMEM((LANES, LANES), F32),
        ],
        compiler_params=pltpu.CompilerParams(
            dimension_semantics=("arbitrary", "arbitrary", "arbitrary"),
            vmem_limit_bytes=VMEM_LIMIT),
        name="attn_prompt",
    )(q, k, v)


def _attn_sample_kernel(pt_ref, q_ref, kn_ref, vn_ref, *rest, n_pages, n_new):
    del pt_ref
    k_refs = rest[:n_pages]
    v_refs = rest[n_pages:2 * n_pages]
    o_ref = rest[2 * n_pages]
    pages_per_blk = MOBA_BLOCK // PAGE_SIZE
    n_blk = n_pages // pages_per_blk
    rows = N_HEADS * n_new

    q = q_ref[...]
    r = lax.broadcasted_iota(jnp.int32, q.shape, 0)
    lane = lax.broadcasted_iota(jnp.int32, q.shape, 1)
    diag = (lane // HEAD_DIM) == (r // n_new)
    qbd = jnp.where(diag, q, jnp.zeros_like(q))
    qf = qbd.astype(F32)

    kb = []
    sb = []
    for j in range(n_blk):
        pages = [k_refs[j * pages_per_blk + i][...] for i in range(pages_per_blk)]
        ksum = pages[0].sum(axis=0, keepdims=True)
        for pg in pages[1:]:
            ksum = ksum + pg.sum(axis=0, keepdims=True)
        kmean = ksum * (1.0 / MOBA_BLOCK)
        sb.append(jnp.sum(qf * kmean, axis=-1, keepdims=True))
        kb.append(jnp.concatenate([pg.astype(BF16) for pg in pages], axis=0))

    sel = []
    for j in range(n_blk):
        rank = jnp.zeros(sb[j].shape, jnp.int32)
        for i in range(n_blk):
            if i == j:
                continue
            beats = (sb[i] >= sb[j]) if i < j else (sb[i] > sb[j])
            rank = rank + beats.astype(jnp.int32)
        sel.append(rank < MOBA_TOPK)

    tok = r[:, 0:1] % n_new
    s_past = [jnp.where(sel[j], _dot_nt(qbd, kb[j]), MASK_VALUE) for j in range(n_blk)]
    kn = kn_ref[...]
    vn = vn_ref[...]
    s_new = [jnp.where(tok >= i, jnp.sum(qf * kn[i:i + 1, :], axis=-1, keepdims=True), MASK_VALUE)
             for i in range(n_new)]
    m = s_new[0]
    for s in s_new[1:]:
        m = jnp.maximum(m, s)
    for s in s_past:
        m = jnp.maximum(m, jnp.max(s, axis=-1, keepdims=True))

    l = jnp.zeros(m.shape, F32)
    acc = jnp.zeros(q.shape, F32)
    for i in range(n_new):
        p = jnp.exp(s_new[i] - m)
        l = l + p
        acc = acc + p * vn[i:i + 1, :]
    for j in range(n_blk):
        p = jnp.exp(s_past[j] - m)
        l = l + jnp.sum(p, axis=-1, keepdims=True)
        vb = jnp.concatenate(
            [v_refs[j * pages_per_blk + i][...].astype(BF16) for i in range(pages_per_blk)], axis=0)
        acc = acc + _dot(p.astype(BF16), vb)
    o = jnp.where(diag, acc / l, 0.0)
    while o.shape[0] > 8:
        half = o.shape[0] // 2
        o = o[:half] + o[half:]
    shift = 4
    while shift >= n_new:
        o = o + pltpu.roll(o, shift, axis=0)
        shift //= 2
    o_ref[...] = o.astype(o_ref.dtype)


def _attn_sample(q, k_new, v_new, cache_k, cache_v, page_table):
    n, t, w = q.shape
    n_pages = page_table.shape[1]
    assert (n_pages * PAGE_SIZE) % MOBA_BLOCK == 0 and MOBA_BLOCK % PAGE_SIZE == 0
    assert t in (1, 2, 4, 8) and n_pages * PAGE_SIZE // MOBA_BLOCK >= MOBA_TOPK
    n_pool = cache_k.shape[0]
    ck = cache_k.reshape(n_pool, PAGE_SIZE, w)
    cv = cache_v.reshape(n_pool, PAGE_SIZE, w)
    q_rep = jnp.tile(q, (1, N_HEADS, 1))
    pad = ((0, 0), (0, 8 - t), (0, 0))
    kn = jnp.pad(k_new, pad)
    vn = jnp.pad(v_new, pad)

    def page_spec(p):
        return pl.BlockSpec((None, PAGE_SIZE, w), lambda i, pt: (pt[i * n_pages + p], 0, 0))

    grid_spec = pltpu.PrefetchScalarGridSpec(
        num_scalar_prefetch=1,
        grid=(n,),
        in_specs=[
            pl.BlockSpec((None, N_HEADS * t, w), lambda i, pt: (i, 0, 0)),
            pl.BlockSpec((None, 8, w), lambda i, pt: (i, 0, 0)),
            pl.BlockSpec((None, 8, w), lambda i, pt: (i, 0, 0)),
        ] + [page_spec(p) for p in range(n_pages)] * 2,
        out_specs=pl.BlockSpec((None, 8, w), lambda i, pt: (i, 0, 0)),
    )
    out = pl.pallas_call(
        functools.partial(_attn_sample_kernel, n_pages=n_pages, n_new=t),
        grid_spec=grid_spec,
        out_shape=jax.ShapeDtypeStruct((n, 8, w), BF16),
        compiler_params=pltpu.CompilerParams(
            dimension_semantics=("arbitrary",), vmem_limit_bytes=VMEM_LIMIT),
        name="attn_sample",
    )(page_table.reshape(-1), q_rep, kn, vn, *([ck] * n_pages), *([cv] * n_pages))
    return out[:, :t]


def _merge_ffn_kernel(x_ref, ao_ref, co_ref, ga_ref, gc_ref, wpa_ref, wpc_ref, wo_ref,
                      g1_ref, g2_ref, g3_ref, wg_ref, wu_ref, wd_ref, y_ref, a_ref, *, ff_chunk):
    m = ga_ref[...] * _dot(ao_ref[...], wpa_ref[...]) + gc_ref[...] * _dot(co_ref[...], wpc_ref[...])
    h = x_ref[...] + _rms(_dot(m.astype(BF16), wo_ref[...])) * g1_ref[...]
    hn = (_rms(h) * g2_ref[...]).astype(BF16)
    d_ff = wg_ref.shape[1]
    for c in range(0, d_ff, ff_chunk):
        g = _dot(hn, wg_ref[:, c:c + ff_chunk])
        u = _dot(hn, wu_ref[:, c:c + ff_chunk])
        a_ref[:, c:c + ff_chunk] = (g * jax.nn.sigmoid(g) * u).astype(BF16)
    f = _dot(a_ref[...], wd_ref[...])
    y_ref[...] = h + _rms(f) * g3_ref[...]


def _merge_ffn(x, ao, co, ga, gc, wpa, wpc, wo, g1, g2, g3, wg, wu, wd, tm):
    rows, d = x.shape
    cw = ao.shape[1]
    d_ff = wg.shape[1]
    ff_chunk = 256
    assert d_ff % ff_chunk == 0 and rows % tm == 0
    row_blk = lambda width: pl.BlockSpec((tm, width), lambda i: (i, 0))
    return pl.pallas_call(
        functools.partial(_merge_ffn_kernel, ff_chunk=ff_chunk),
        grid=(rows // tm,),
        in_specs=[row_blk(d), row_blk(cw), row_blk(cw), row_blk(d), row_blk(d),
                  _resident(wpa.shape), _resident(wpc.shape), _resident(wo.shape),
                  _resident((1, d)), _resident((1, d)), _resident((1, d)),
                  _resident(wg.shape), _resident(wu.shape), _resident(wd.shape)],
        out_specs=row_blk(d),
        out_shape=jax.ShapeDtypeStruct((rows, d), F32),
        scratch_shapes=[pltpu.VMEM((tm, d_ff), BF16)],
        compiler_params=pltpu.CompilerParams(
            dimension_semantics=("arbitrary",), vmem_limit_bytes=VMEM_LIMIT),
        name="merge_ffn",
    )(x, ao, co, ga, gc, wpa, wpc, wo, g1.reshape(1, d), g2.reshape(1, d), g3.reshape(1, d),
      wg, wu, wd)


def kernel(x_prompt, x_sample, cache_k, cache_v, state_conv, page_table, g_attn_pre, w_in, w_conv, w_proj_attn, w_proj_conv, w_out, g_attn_post, g_ffn_pre, w_gate, w_up, w_down, g_ffn_post):
    depth = w_in.shape[0]
    bsz, s, d = x_prompt.shape
    n, t, _ = x_sample.shape
    cw = ATTN_WIDTH
    assert t >= CONV_WIDTH - 1 and s >= CONV_WIDTH - 1
    xp, xs = x_prompt, x_sample
    outs = [[] for _ in range(6)]
    for l in range(depth):
        w_in_b = w_in[l].astype(BF16)
        tail = (w_proj_attn[l].astype(BF16), w_proj_conv[l].astype(BF16), w_out[l].astype(BF16),
                g_attn_post[l], g_ffn_pre[l], g_ffn_post[l],
                w_gate[l].astype(BF16), w_up[l].astype(BF16), w_down[l].astype(BF16))

        prev0 = jnp.zeros((bsz, CONV_WIDTH - 1, cw), F32)
        q, k, v, co, ga, gc, u_tail = _proj_prompt(xp, g_attn_pre[l], w_in_b, w_conv[l], prev0, 512)
        ao = _attn_prompt(q, k, v)
        flat = lambda a: a.reshape(bsz * s, a.shape[-1])
        xp = _merge_ffn(flat(xp), flat(ao), flat(co), flat(ga), flat(gc), *tail, 256)
        xp = xp.reshape(bsz, s, d)
        outs[0].append(k.reshape(bsz, s, N_HEADS, HEAD_DIM))
        outs[1].append(v.reshape(bsz, s, N_HEADS, HEAD_DIM))
        outs[2].append(u_tail[:, 8 - (CONV_WIDTH - 1):])

        q, k, v, co, ga, gc, u = _proj_sample(xs, g_attn_pre[l], w_in_b, w_conv[l], state_conv[l])
        ao = _attn_sample(q.reshape(n, t, cw), k.reshape(n, t, cw), v.reshape(n, t, cw),
                          cache_k[l], cache_v[l], page_table)
        xs = _merge_ffn(xs.reshape(n * t, d), ao.reshape(n * t, cw), co, ga, gc, *tail, 256)
        xs = xs.reshape(n, t, d)
        outs[3].append(k.reshape(n, t, N_HEADS, HEAD_DIM))
        outs[4].append(v.reshape(n, t, N_HEADS, HEAD_DIM))
        outs[5].append(u.reshape(n, t, cw)[:, t - (CONV_WIDTH - 1):])

    kp, vp, cp, ks, vs, cs = (jnp.stack(o) for o in outs)
    return (xp, xs, kp, vp, cp, ks, vs, cs)
```

```python
import functools

import jax
import jax.numpy as jnp
from jax import lax
from jax.experimental import pallas as pl
from jax.experimental.pallas import tpu as pltpu

N_HEADS = 8
HEAD_DIM = 64
ATTN_WIDTH = N_HEADS * HEAD_DIM
MOBA_BLOCK = 256
MOBA_TOPK = 3
CONV_WIDTH = 3
PAGE_SIZE = 128
RMS_EPS = 1e-6

LANES = 128
HEADS_PER_GROUP = LANES // HEAD_DIM
MASK_VALUE = -(2.0 ** 100)
Q_SCALE = HEAD_DIM ** -0.5 * 1.4426950408889634
VMEM_LIMIT = 52 * 1024 * 1024

F32 = jnp.float32
BF16 = jnp.bfloat16


def _dot(a, b):
    return jnp.dot(a, b, preferred_element_type=F32)


def _dot_nt(a, b):
    return lax.dot_general(a, b, (((1,), (1,)), ((), ())), preferred_element_type=F32)


def _rms(x):
    return x * lax.rsqrt(jnp.mean(x * x, axis=-1, keepdims=True) + RMS_EPS)


def _resident(shape):
    return pl.BlockSpec(shape, lambda *_: (0,) * len(shape), pipeline_mode=pl.Buffered(1))


def _proj_kernel(x_ref, g_ref, w_ref, wc_ref, b1_ref, b2_ref,
                 q_ref, k_ref, v_ref, co_ref, ga_ref, gc_ref, u_ref,
                 carry_ref, *, seq_rows, d_model):
    tm = x_ref.shape[0]
    cw = ATTN_WIDTH
    xn = (_rms(x_ref[...]) * g_ref[...]).astype(BF16)

    def proj(lo, width):
        return _dot(xn, w_ref[:, lo:lo + width])

    q_ref[...] = (proj(0, cw) * Q_SCALE).astype(BF16)
    k_ref[...] = proj(cw, cw)
    v_ref[...] = proj(2 * cw, cw)
    cc = proj(4 * cw, cw)
    cx = proj(5 * cw, cw)
    u = cc * cx
    row = lax.broadcasted_iota(jnp.int32, u.shape, 0)
    s1 = pltpu.roll(u, 1, axis=0)
    s2 = pltpu.roll(u, 2, axis=0)
    if seq_rows is None:
        @pl.when(pl.program_id(1) == 0)
        def _():
            carry_ref[...] = b1_ref[0]
        c = carry_ref[...]
        prev1 = jnp.broadcast_to(c[7:8, :], u.shape)
        prev2 = jnp.broadcast_to(c[6:7, :], u.shape)
        s1 = jnp.where(row == 0, prev1, s1)
        s2 = jnp.where(row == 0, prev2, jnp.where(row == 1, prev1, s2))
        carry_ref[...] = u[tm - 8:tm, :]
        u_ref[0] = u[tm - 8:tm, :]
    else:
        t = row % seq_rows
        s1 = jnp.where(t >= 1, s1, b1_ref[...])
        s2 = jnp.where(t >= 2, s2, b2_ref[...])
        u_ref[...] = u
    wc = wc_ref[...]
    y = wc[0:1, :] * s2 + wc[1:2, :] * s1 + wc[2:3, :] * u
    co_ref[...] = (proj(3 * cw, cw) * y).astype(BF16)
    ga_ref[...] = jax.nn.sigmoid(proj(6 * cw, d_model))
    gc_ref[...] = jax.nn.sigmoid(proj(6 * cw + d_model, d_model))


def _proj_prompt(x, g, w_in, w_conv, prev, tm):
    bsz, s, d = x.shape
    cw = ATTN_WIDTH
    hist = jnp.zeros((bsz, 8, cw), F32).at[:, 8 - (CONV_WIDTH - 1):].set(prev)
    row_blk = lambda width: pl.BlockSpec((None, tm, width), lambda b, i: (b, i, 0))
    outs = pl.pallas_call(
        functools.partial(_proj_kernel, seq_rows=None, d_model=d),
        grid=(bsz, s // tm),
        in_specs=[
            row_blk(d),
            _resident((1, d)),
            _resident(w_in.shape),
            _resident(w_conv.shape),
            pl.BlockSpec((1, 8, cw), lambda b, i: (b, 0, 0)),
            pl.BlockSpec((1, 8, cw), lambda b, i: (b, 0, 0)),
        ],
        out_specs=[row_blk(cw), row_blk(cw), row_blk(cw), row_blk(cw), row_blk(d), row_blk(d),
                   pl.BlockSpec((1, 8, cw), lambda b, i: (b, 0, 0))],
        out_shape=[
            jax.ShapeDtypeStruct((bsz, s, cw), BF16),
            jax.ShapeDtypeStruct((bsz, s, cw), F32),
            jax.ShapeDtypeStruct((bsz, s, cw), F32),
            jax.ShapeDtypeStruct((bsz, s, cw), BF16),
            jax.ShapeDtypeStruct((bsz, s, d), F32),
            jax.ShapeDtypeStruct((bsz, s, d), F32),
            jax.ShapeDtypeStruct((bsz, 8, cw), F32),
        ],
        scratch_shapes=[pltpu.VMEM((8, cw), F32)],
        compiler_params=pltpu.CompilerParams(
            dimension_semantics=("arbitrary", "arbitrary"), vmem_limit_bytes=VMEM_LIMIT),
        name="proj_prompt",
    )(x, g.reshape(1, d), w_in, w_conv, hist, hist)
    return outs


def _proj_sample(x, g, w_in, w_conv, state):
    n, t, d = x.shape
    cw = ATTN_WIDTH
    rows = n * t
    b1 = jnp.zeros((n, t, cw), F32).at[:, 0].set(state[:, 1]).reshape(rows, cw)
    b2 = jnp.zeros((n, t, cw), F32).at[:, 0].set(state[:, 0]).at[:, 1].set(state[:, 1])
    b2 = b2.reshape(rows, cw)
    full = lambda width: pl.BlockSpec((rows, width), lambda i: (0, 0))
    outs = pl.pallas_call(
        functools.partial(_proj_kernel, seq_rows=t, d_model=d),
        grid=(1,),
        in_specs=[full(d), _resident((1, d)), _resident(w_in.shape), _resident(w_conv.shape),
                  full(cw), full(cw)],
        out_specs=[full(cw), full(cw), full(cw), full(cw), full(d), full(d), full(cw)],
        out_shape=[
            jax.ShapeDtypeStruct((rows, cw), BF16),
            jax.ShapeDtypeStruct((rows, cw), F32),
            jax.ShapeDtypeStruct((rows, cw), F32),
            jax.ShapeDtypeStruct((rows, cw), BF16),
            jax.ShapeDtypeStruct((rows, d), F32),
            jax.ShapeDtypeStruct((rows, d), F32),
            jax.ShapeDtypeStruct((rows, cw), F32),
        ],
        scratch_shapes=[pltpu.VMEM((8, cw), F32)],
        compiler_params=pltpu.CompilerParams(
            dimension_semantics=("arbitrary",), vmem_limit_bytes=VMEM_LIMIT),
        name="proj_sample",
    )(x.reshape(rows, d), g.reshape(1, d), w_in, w_conv, b1, b2)
    return outs


def _select_blocks(scores, n_valid):
    blk_id = lax.broadcasted_iota(jnp.int32, scores.shape, 0)
    blk_f = blk_id.astype(F32)
    cand = jnp.where(blk_id < n_valid, scores, -jnp.inf)
    sel = jnp.zeros(scores.shape, jnp.bool_)
    for _ in range(MOBA_TOPK):
        mx = jnp.max(cand, axis=0, keepdims=True)
        idx = jnp.min(jnp.where(cand == mx, blk_f, float(scores.shape[0])), axis=0, keepdims=True)
        pick = (blk_f == idx) & (mx > -jnp.inf)
        sel = sel | pick
        cand = jnp.where(pick, -jnp.inf, cand)
    return sel


def _attn_prompt_kernel(q_ref, k_ref, v_ref, o_ref, kaug_ref, vt_ref, kmean_ref, s_ref, *,
                        n_blocks, chunk):
    t = pl.program_id(2)
    blk = MOBA_BLOCK

    @pl.when(t == 0)
    def _init():
        lane_k = lax.broadcasted_iota(jnp.int32, (blk, LANES), 1)

        def fill(j, c):
            rows = pl.ds(pl.multiple_of(j * blk, blk), blk)
            kb = k_ref[rows, :]
            kaug_ref[rows, 0:LANES] = kb.astype(BF16)
            kaug_ref[rows, LANES:2 * LANES] = (lane_k == j).astype(BF16)
            vt_ref[j] = v_ref[rows, :].T.astype(BF16)
            kmean_ref[pl.ds(j, 1), :] = jnp.sum(kb, axis=0, keepdims=True) * (1.0 / blk)
            return c

        lax.fori_loop(0, n_blocks, fill, 0)

    q_t = q_ref[...].astype(F32).T
    d_row = lax.broadcasted_iota(jnp.int32, q_t.shape, 0)
    kmean = kmean_ref[...].astype(BF16)
    key_i = lax.broadcasted_iota(jnp.int32, (blk, blk), 0)
    qry_i = lax.broadcasted_iota(jnp.int32, (blk, blk), 1)
    causal = key_i <= qry_i
    own = pl.ds(pl.multiple_of(t * blk, blk), blk)
    k_own = kaug_ref[own, 0:LANES]
    v_own = vt_ref[t]
    pad_rows = jnp.zeros((LANES - n_blocks, blk), BF16)

    def head_rows(a, h):
        return a[h * HEAD_DIM:(h + 1) * HEAD_DIM, :]

    q_aug = []
    init = []
    for h in range(HEADS_PER_GROUP):
        in_head = (d_row >= h * HEAD_DIM) & (d_row < (h + 1) * HEAD_DIM)
        qh = jnp.where(in_head, q_t, 0.0).astype(BF16)
        sel = _select_blocks(_dot(kmean, qh), t)
        pen = jnp.where(sel, 0.0, MASK_VALUE).astype(BF16)
        q_aug.append(jnp.concatenate([qh, pen, pad_rows], axis=0))
        s = jnp.where(causal, _dot(k_own, qh), MASK_VALUE)
        m = jnp.max(s, axis=0, keepdims=True)
        p = jnp.exp2(s - m)
        init.append((m, jnp.sum(p, axis=0, keepdims=True),
                     _dot(head_rows(v_own, h), p.astype(BF16))))

    n_chunks = (t + chunk - 1) // chunk

    last_chunk = n_blocks // chunk - 1

    def put_scores(c, slot):
        rows = pl.ds(pl.multiple_of(c * (chunk * blk), chunk * blk), chunk * blk)
        kc = kaug_ref[rows, :]
        for h in range(HEADS_PER_GROUP):
            s_ref[slot, h] = _dot(kc, q_aug[h])

    def consume(c, slot, state):
        new = []
        for h in range(HEADS_PER_GROUP):
            m, l, acc = state[h]
            s = s_ref[slot, h]
            m_new = jnp.maximum(m, jnp.max(s, axis=0, keepdims=True))
            a = jnp.exp2(m - m_new)
            p = jnp.exp2(s - m_new)
            l = a * l + jnp.sum(p, axis=0, keepdims=True)
            pb = p.astype(BF16)
            acc = a * acc
            for i in range(chunk):
                acc = acc + _dot(head_rows(vt_ref[c * chunk + i], h), pb[i * blk:(i + 1) * blk, :])
            new.append((m_new, l, acc))
        return tuple(new)

    def past_pair(i, state):
        put_scores(2 * i + 1, 1)
        state = consume(2 * i, 0, state)
        put_scores(jnp.minimum(2 * i + 2, last_chunk), 0)
        return consume(2 * i + 1, 1, state)

    put_scores(0, 0)
    fin = lax.fori_loop(0, (n_chunks + 1) // 2, past_pair, tuple(init))
    o_t = jnp.concatenate([acc / l for (_, l, acc) in fin], axis=0)
    o_ref[...] = o_t.T.astype(o_ref.dtype)


def _attn_prompt(q, k, v):
    bsz, s, w = q.shape
    n_blocks = s // MOBA_BLOCK
    chunk = 4
    assert n_blocks % (2 * chunk) == 0 and n_blocks % 16 == 0 and n_blocks <= LANES
    groups = w // LANES
    return pl.pallas_call(
        functools.partial(_attn_prompt_kernel, n_blocks=n_blocks, chunk=chunk),
        grid=(bsz, groups, n_blocks),
        in_specs=[
            pl.BlockSpec((None, MOBA_BLOCK, LANES), lambda b, g, t: (b, t, g)),
            pl.BlockSpec((None, s, LANES), lambda b, g, t: (b, 0, g)),
            pl.BlockSpec((None, s, LANES), lambda b, g, t: (b, 0, g)),
        ],
        out_specs=pl.BlockSpec((None, MOBA_BLOCK, LANES), lambda b, g, t: (b, t, g)),
        out_shape=jax.ShapeDtypeStruct((bsz, s, w), BF16),
        scratch_shapes=[
            pltpu.VMEM((s, 2 * LANES), BF16),
            pltpu.VMEM((n_blocks, LANES, MOBA_BLOCK), BF16),
            pltpu.VMEM((n_blocks, LANES), F32),
            pltpu.VMEM((2, HEADS_PER_GROUP, chunk * MOBA_BLOCK, MOBA_BLOCK), F32),
        ],
        compiler_params=pltpu.CompilerParams(
            dimension_semantics=("arbitrary", "arbitrary", "arbitrary"),
            vmem_limit_bytes=VMEM_LIMIT),
        name="attn_prompt",
    )(q, k, v)


def _attn_sample_kernel(pt_ref, q_ref, kn_ref, vn_ref, *rest, n_pages, n_new):
    del pt_ref
    k_refs = rest[:n_pages]
    v_refs = rest[n_pages:2 * n_pages]
    o_ref = rest[2 * n_pages]
    pages_per_blk = MOBA_BLOCK // PAGE_SIZE
    n_blk = n_pages // pages_per_blk
    rows = N_HEADS * n_new

    q = q_ref[...]
    r = lax.broadcasted_iota(jnp.int32, q.shape, 0)
    lane = lax.broadcasted_iota(jnp.int32, q.shape, 1)
    diag = (lane // HEAD_DIM) == (r // n_new)
    qbd = jnp.where(diag, q, jnp.zeros_like(q))
    qf = qbd.astype(F32)

    kb = []
    sb = []
    for j in range(n_blk):
        pages = [k_refs[j * pages_per_blk + i][...] for i in range(pages_per_blk)]
        ksum = pages[0].sum(axis=0, keepdims=True)
        for pg in pages[1:]:
            ksum = ksum + pg.sum(axis=0, keepdims=True)
        kmean = ksum * (1.0 / MOBA_BLOCK)
        sb.append(jnp.sum(qf * kmean, axis=-1, keepdims=True))
        kb.append(jnp.concatenate([pg.astype(BF16) for pg in pages], axis=0))

    sel = []
    for j in range(n_blk):
        rank = jnp.zeros(sb[j].shape, jnp.int32)
        for i in range(n_blk):
            if i == j:
                continue
            beats = (sb[i] >= sb[j]) if i < j else (sb[i] > sb[j])
            rank = rank + beats.astype(jnp.int32)
        sel.append(rank < MOBA_TOPK)

    tok = r[:, 0:1] % n_new
    s_past = [jnp.where(sel[j], _dot_nt(qbd, kb[j]), MASK_VALUE) for j in range(n_blk)]
    kn = kn_ref[...]
    vn = vn_ref[...]
    s_new = [jnp.where(tok >= i, jnp.sum(qf * kn[i:i + 1, :], axis=-1, keepdims=True), MASK_VALUE)
             for i in range(n_new)]
    m = s_new[0]
    for s in s_new[1:]:
        m = jnp.maximum(m, s)
    for s in s_past:
        m = jnp.maximum(m, jnp.max(s, axis=-1, keepdims=True))

    l = jnp.zeros(m.shape, F32)
    acc = jnp.zeros(q.shape, F32)
    for i in range(n_new):
        p = jnp.exp2(s_new[i] - m)
        l = l + p
        acc = acc + p * vn[i:i + 1, :]
    for j in range(n_blk):
        p = jnp.exp2(s_past[j] - m)
        l = l + jnp.sum(p, axis=-1, keepdims=True)
        vb = jnp.concatenate(
            [v_refs[j * pages_per_blk + i][...].astype(BF16) for i in range(pages_per_blk)], axis=0)
        acc = acc + _dot(p.astype(BF16), vb)
    o = jnp.where(diag, acc / l, 0.0)
    while o.shape[0] > 8:
        half = o.shape[0] // 2
        o = o[:half] + o[half:]
    shift = 4
    while shift >= n_new:
        o = o + pltpu.roll(o, shift, axis=0)
        shift //= 2
    o_ref[...] = o.astype(o_ref.dtype)


def _attn_sample(q, k_new, v_new, cache_k, cache_v, page_table):
    n, t, w = q.shape
    n_pages = page_table.shape[1]
    assert (n_pages * PAGE_SIZE) % MOBA_BLOCK == 0 and MOBA_BLOCK % PAGE_SIZE == 0
    assert t in (1, 2, 4, 8) and n_pages * PAGE_SIZE // MOBA_BLOCK >= MOBA_TOPK
    n_pool = cache_k.shape[0]
    ck = cache_k.reshape(n_pool, PAGE_SIZE, w)
    cv = cache_v.reshape(n_pool, PAGE_SIZE, w)
    q_rep = jnp.tile(q, (1, N_HEADS, 1))
    pad = ((0, 0), (0, 8 - t), (0, 0))
    kn = jnp.pad(k_new, pad)
    vn = jnp.pad(v_new, pad)

    def page_spec(p):
        return pl.BlockSpec((None, PAGE_SIZE, w), lambda i, pt: (pt[i * n_pages + p], 0, 0))

    grid_spec = pltpu.PrefetchScalarGridSpec(
        num_scalar_prefetch=1,
        grid=(n,),
        in_specs=[
            pl.BlockSpec((None, N_HEADS * t, w), lambda i, pt: (i, 0, 0)),
            pl.BlockSpec((None, 8, w), lambda i, pt: (i, 0, 0)),
            pl.BlockSpec((None, 8, w), lambda i, pt: (i, 0, 0)),
        ] + [page_spec(p) for p in range(n_pages)] * 2,
        out_specs=pl.BlockSpec((None, 8, w), lambda i, pt: (i, 0, 0)),
    )
    out = pl.pallas_call(
        functools.partial(_attn_sample_kernel, n_pages=n_pages, n_new=t),
        grid_spec=grid_spec,
        out_shape=jax.ShapeDtypeStruct((n, 8, w), BF16),
        compiler_params=pltpu.CompilerParams(
            dimension_semantics=("arbitrary",), vmem_limit_bytes=VMEM_LIMIT),
        name="attn_sample",
    )(page_table.reshape(-1), q_rep, kn, vn, *([ck] * n_pages), *([cv] * n_pages))
    return out[:, :t]


def _merge_ffn_kernel(x_ref, ao_ref, co_ref, ga_ref, gc_ref, wpa_ref, wpc_ref, wo_ref,
                      g1_ref, g2_ref, g3_ref, wg_ref, wu_ref, wd_ref, y_ref, a_ref, *, ff_chunk):
    m = ga_ref[...] * _dot(ao_ref[...], wpa_ref[...]) + gc_ref[...] * _dot(co_ref[...], wpc_ref[...])
    h = x_ref[...] + _rms(_dot(m.astype(BF16), wo_ref[...])) * g1_ref[...]
    hn = (_rms(h) * g2_ref[...]).astype(BF16)
    d_ff = wg_ref.shape[1]
    for c in range(0, d_ff, ff_chunk):
        g = _dot(hn, wg_ref[:, c:c + ff_chunk])
        u = _dot(hn, wu_ref[:, c:c + ff_chunk])
        a_ref[:, c:c + ff_chunk] = (g * jax.nn.sigmoid(g) * u).astype(BF16)
    f = _dot(a_ref[...], wd_ref[...])
    y_ref[...] = h + _rms(f) * g3_ref[...]


def _merge_ffn(x, ao, co, ga, gc, wpa, wpc, wo, g1, g2, g3, wg, wu, wd, tm):
    rows, d = x.shape
    cw = ao.shape[1]
    d_ff = wg.shape[1]
    ff_chunk = 256
    assert d_ff % ff_chunk == 0 and rows % tm == 0
    row_blk = lambda width: pl.BlockSpec((tm, width), lambda i: (i, 0))
    return pl.pallas_call(
        functools.partial(_merge_ffn_kernel, ff_chunk=ff_chunk),
        grid=(rows // tm,),
        in_specs=[row_blk(d), row_blk(cw), row_blk(cw), row_blk(d), row_blk(d),
                  _resident(wpa.shape), _resident(wpc.shape), _resident(wo.shape),
                  _resident((1, d)), _resident((1, d)), _resident((1, d)),
                  _resident(wg.shape), _resident(wu.shape), _resident(wd.shape)],
        out_specs=row_blk(d),
        out_shape=jax.ShapeDtypeStruct((rows, d), F32),
        scratch_shapes=[pltpu.VMEM((tm, d_ff), BF16)],
        compiler_params=pltpu.CompilerParams(
            dimension_semantics=("arbitrary",), vmem_limit_bytes=VMEM_LIMIT),
        name="merge_ffn",
    )(x, ao, co, ga, gc, wpa, wpc, wo, g1.reshape(1, d), g2.reshape(1, d), g3.reshape(1, d),
      wg, wu, wd)


def kernel(x_prompt, x_sample, cache_k, cache_v, state_conv, page_table, g_attn_pre, w_in, w_conv, w_proj_attn, w_proj_conv, w_out, g_attn_post, g_ffn_pre, w_gate, w_up, w_down, g_ffn_post):
    depth = w_in.shape[0]
    bsz, s, d = x_prompt.shape
    n, t, _ = x_sample.shape
    cw = ATTN_WIDTH
    assert t >= CONV_WIDTH - 1 and s >= CONV_WIDTH - 1
    xp, xs = x_prompt, x_sample
    outs = [[] for _ in range(6)]
    for l in range(depth):
        w_in_b = w_in[l].astype(BF16)
        tail = (w_proj_attn[l].astype(BF16), w_proj_conv[l].astype(BF16), w_out[l].astype(BF16),
                g_attn_post[l], g_ffn_pre[l], g_ffn_post[l],
                w_gate[l].astype(BF16), w_up[l].astype(BF16), w_down[l].astype(BF16))

        prev0 = jnp.zeros((bsz, CONV_WIDTH - 1, cw), F32)
        q, k, v, co, ga, gc, u_tail = _proj_prompt(xp, g_attn_pre[l], w_in_b, w_conv[l], prev0, 512)
        ao = _attn_prompt(q, k, v)
        flat = lambda a: a.reshape(bsz * s, a.shape[-1])
        xp = _merge_ffn(flat(xp), flat(ao), flat(co), flat(ga), flat(gc), *tail, 256)
        xp = xp.reshape(bsz, s, d)
        outs[0].append(k.reshape(bsz, s, N_HEADS, HEAD_DIM))
        outs[1].append(v.reshape(bsz, s, N_HEADS, HEAD_DIM))
        outs[2].append(u_tail[:, 8 - (CONV_WIDTH - 1):])

        q, k, v, co, ga, gc, u = _proj_sample(xs, g_attn_pre[l], w_in_b, w_conv[l], state_conv[l])
        ao = _attn_sample(q.reshape(n, t, cw), k.reshape(n, t, cw), v.reshape(n, t, cw),
                          cache_k[l], cache_v[l], page_table)
        xs = _merge_ffn(xs.reshape(n * t, d), ao.reshape(n * t, cw), co, ga, gc, *tail, 256)
        xs = xs.reshape(n, t, d)
        outs[3].append(k.reshape(n, t, N_HEADS, HEAD_DIM))
        outs[4].append(v.reshape(n, t, N_HEADS, HEAD_DIM))
        outs[5].append(u.reshape(n, t, cw)[:, t - (CONV_WIDTH - 1):])

    kp, vp, cp, ks, vs, cs = (jnp.stack(o) for o in outs)
    return (xp, xs, kp, vp, cp, ks, vs, cs)
```

```python
import functools

import jax
import jax.numpy as jnp
from jax import lax
from jax.experimental import pallas as pl
from jax.experimental.pallas import tpu as pltpu

N_HEADS = 8
HEAD_DIM = 64
ATTN_WIDTH = N_HEADS * HEAD_DIM
MOBA_BLOCK = 256
MOBA_TOPK = 3
CONV_WIDTH = 3
PAGE_SIZE = 128
RMS_EPS = 1e-6

LANES = 128
HEADS_PER_GROUP = LANES // HEAD_DIM
MASK_VALUE = -(2.0 ** 100)
Q_SCALE = HEAD_DIM ** -0.5 * 1.4426950408889634
VMEM_LIMIT = 52 * 1024 * 1024

F32 = jnp.float32
BF16 = jnp.bfloat16


def _dot(a, b):
    return jnp.dot(a, b, preferred_element_type=F32)


def _dot_nt(a, b):
    return lax.dot_general(a, b, (((1,), (1,)), ((), ())), preferred_element_type=F32)


def _rms(x):
    return x * lax.rsqrt(jnp.mean(x * x, axis=-1, keepdims=True) + RMS_EPS)


def _resident(shape):
    return pl.BlockSpec(shape, lambda *_: (0,) * len(shape), pipeline_mode=pl.Buffered(1))


def _proj_kernel(x_ref, g_ref, w_ref, wc_ref, b1_ref, b2_ref,
                 q_ref, k_ref, v_ref, co_ref, ga_ref, gc_ref, u_ref,
                 carry_ref, *, seq_rows, d_model):
    tm = x_ref.shape[0]
    cw = ATTN_WIDTH
    xn = (_rms(x_ref[...]) * g_ref[...]).astype(BF16)

    def proj(lo, width):
        return _dot(xn, w_ref[:, lo:lo + width])

    q_ref[...] = (proj(0, cw) * Q_SCALE).astype(BF16)
    k_ref[...] = proj(cw, cw)
    v_ref[...] = proj(2 * cw, cw)
    cc = proj(4 * cw, cw)
    cx = proj(5 * cw, cw)
    u = cc * cx
    row = lax.broadcasted_iota(jnp.int32, u.shape, 0)
    s1 = pltpu.roll(u, 1, axis=0)
    s2 = pltpu.roll(u, 2, axis=0)
    if seq_rows is None:
        @pl.when(pl.program_id(1) == 0)
        def _():
            carry_ref[...] = b1_ref[0]
        c = carry_ref[...]
        prev1 = jnp.broadcast_to(c[7:8, :], u.shape)
        prev2 = jnp.broadcast_to(c[6:7, :], u.shape)
        s1 = jnp.where(row == 0, prev1, s1)
        s2 = jnp.where(row == 0, prev2, jnp.where(row == 1, prev1, s2))
        carry_ref[...] = u[tm - 8:tm, :]
        u_ref[0] = u[tm - 8:tm, :]
    else:
        t = row % seq_rows
        s1 = jnp.where(t >= 1, s1, b1_ref[...])
        s2 = jnp.where(t >= 2, s2, b2_ref[...])
        u_ref[...] = u
    wc = wc_ref[...]
    y = wc[0:1, :] * s2 + wc[1:2, :] * s1 + wc[2:3, :] * u
    co_ref[...] = (proj(3 * cw, cw) * y).astype(BF16)
    ga_ref[...] = jax.nn.sigmoid(proj(6 * cw, d_model))
    gc_ref[...] = jax.nn.sigmoid(proj(6 * cw + d_model, d_model))


def _proj_prompt(x, g, w_in, w_conv, prev, tm):
    bsz, s, d = x.shape
    cw = ATTN_WIDTH
    hist = jnp.zeros((bsz, 8, cw), F32).at[:, 8 - (CONV_WIDTH - 1):].set(prev)
    row_blk = lambda width: pl.BlockSpec((None, tm, width), lambda b, i: (b, i, 0))
    outs = pl.pallas_call(
        functools.partial(_proj_kernel, seq_rows=None, d_model=d),
        grid=(bsz, s // tm),
        in_specs=[
            row_blk(d),
            _resident((1, d)),
            _resident(w_in.shape),
            _resident(w_conv.shape),
            pl.BlockSpec((1, 8, cw), lambda b, i: (b, 0, 0)),
            pl.BlockSpec((1, 8, cw), lambda b, i: (b, 0, 0)),
        ],
        out_specs=[row_blk(cw), row_blk(cw), row_blk(cw), row_blk(cw), row_blk(d), row_blk(d),
                   pl.BlockSpec((1, 8, cw), lambda b, i: (b, 0, 0))],
        out_shape=[
            jax.ShapeDtypeStruct((bsz, s, cw), BF16),
            jax.ShapeDtypeStruct((bsz, s, cw), F32),
            jax.ShapeDtypeStruct((bsz, s, cw), F32),
            jax.ShapeDtypeStruct((bsz, s, cw), BF16),
            jax.ShapeDtypeStruct((bsz, s, d), F32),
            jax.ShapeDtypeStruct((bsz, s, d), F32),
            jax.ShapeDtypeStruct((bsz, 8, cw), F32),
        ],
        scratch_shapes=[pltpu.VMEM((8, cw), F32)],
        compiler_params=pltpu.CompilerParams(
            dimension_semantics=("arbitrary", "arbitrary"), vmem_limit_bytes=VMEM_LIMIT),
        name="proj_prompt",
    )(x, g.reshape(1, d), w_in, w_conv, hist, hist)
    return outs


def _proj_sample(x, g, w_in, w_conv, state):
    n, t, d = x.shape
    cw = ATTN_WIDTH
    rows = n * t
    b1 = jnp.zeros((n, t, cw), F32).at[:, 0].set(state[:, 1]).reshape(rows, cw)
    b2 = jnp.zeros((n, t, cw), F32).at[:, 0].set(state[:, 0]).at[:, 1].set(state[:, 1])
    b2 = b2.reshape(rows, cw)
    full = lambda width: pl.BlockSpec((rows, width), lambda i: (0, 0))
    outs = pl.pallas_call(
        functools.partial(_proj_kernel, seq_rows=t, d_model=d),
        grid=(1,),
        in_specs=[full(d), _resident((1, d)), _resident(w_in.shape), _resident(w_conv.shape),
                  full(cw), full(cw)],
        out_specs=[full(cw), full(cw), full(cw), full(cw), full(d), full(d), full(cw)],
        out_shape=[
            jax.ShapeDtypeStruct((rows, cw), BF16),
            jax.ShapeDtypeStruct((rows, cw), F32),
            jax.ShapeDtypeStruct((rows, cw), F32),
            jax.ShapeDtypeStruct((rows, cw), BF16),
            jax.ShapeDtypeStruct((rows, d), F32),
            jax.ShapeDtypeStruct((rows, d), F32),
            jax.ShapeDtypeStruct((rows, cw), F32),
        ],
        scratch_shapes=[pltpu.VMEM((8, cw), F32)],
        compiler_params=pltpu.CompilerParams(
            dimension_semantics=("arbitrary",), vmem_limit_bytes=VMEM_LIMIT),
        name="proj_sample",
    )(x.reshape(rows, d), g.reshape(1, d), w_in, w_conv, b1, b2)
    return outs


def _select_blocks(scores, n_valid):
    blk_id = lax.broadcasted_iota(jnp.int32, scores.shape, 0)
    blk_f = blk_id.astype(F32)
    cand = jnp.where(blk_id < n_valid, scores, -jnp.inf)
    sel = jnp.zeros(scores.shape, jnp.bool_)
    for _ in range(MOBA_TOPK):
        mx = jnp.max(cand, axis=0, keepdims=True)
        idx = jnp.min(jnp.where(cand == mx, blk_f, float(scores.shape[0])), axis=0, keepdims=True)
        pick = (blk_f == idx) & (mx > -jnp.inf)
        sel = sel | pick
        cand = jnp.where(pick, -jnp.inf, cand)
    return sel


def _attn_prompt_kernel(q_ref, k_ref, v_ref, o_ref, kaug_ref, vt_ref, kmean_ref, s_ref, *,
                        n_blocks, chunk):
    t = pl.program_id(2)
    blk = MOBA_BLOCK

    @pl.when(t == 0)
    def _init():
        lane_k = lax.broadcasted_iota(jnp.int32, (blk, LANES), 1)

        def fill(j, c):
            rows = pl.ds(pl.multiple_of(j * blk, blk), blk)
            kb = k_ref[rows, :]
            kaug_ref[rows, 0:LANES] = kb.astype(BF16)
            kaug_ref[rows, LANES:2 * LANES] = (lane_k == j).astype(BF16)
            vt_ref[j] = v_ref[rows, :].T.astype(BF16)
            kmean_ref[pl.ds(j, 1), :] = jnp.sum(kb, axis=0, keepdims=True) * (1.0 / blk)
            return c

        lax.fori_loop(0, n_blocks, fill, 0)

    q_t = q_ref[...].astype(F32).T
    d_row = lax.broadcasted_iota(jnp.int32, q_t.shape, 0)
    kmean = kmean_ref[...].astype(BF16)
    key_i = lax.broadcasted_iota(jnp.int32, (blk, blk), 0)
    qry_i = lax.broadcasted_iota(jnp.int32, (blk, blk), 1)
    causal = key_i <= qry_i
    own = pl.ds(pl.multiple_of(t * blk, blk), blk)
    k_own = kaug_ref[own, 0:LANES]
    v_own = vt_ref[t]
    pad_rows = jnp.zeros((LANES - n_blocks, blk), BF16)

    def head_rows(a, h):
        return a[h * HEAD_DIM:(h + 1) * HEAD_DIM, :]

    q_aug = []
    init = []
    for h in range(HEADS_PER_GROUP):
        in_head = (d_row >= h * HEAD_DIM) & (d_row < (h + 1) * HEAD_DIM)
        qh = jnp.where(in_head, q_t, 0.0).astype(BF16)
        sel = _select_blocks(_dot(kmean, qh), t)
        pen = jnp.where(sel, 0.0, MASK_VALUE).astype(BF16)
        q_aug.append(jnp.concatenate([qh, pen, pad_rows], axis=0))
        s = jnp.where(causal, _dot(k_own, qh), MASK_VALUE)
        m = jnp.max(s, axis=0, keepdims=True)
        p = jnp.exp2(s - m)
        init.append((m, jnp.sum(p, axis=0, keepdims=True),
                     _dot(head_rows(v_own, h), p.astype(BF16))))

    n_chunks = (t + chunk - 1) // chunk

    last_chunk = n_blocks // chunk - 1

    def put_scores(c, slot):
        rows = pl.ds(pl.multiple_of(c * (chunk * blk), chunk * blk), chunk * blk)
        kc = kaug_ref[rows, :]
        for h in range(HEADS_PER_GROUP):
            s_ref[slot, h] = _dot(kc, q_aug[h])

    def consume(c, slot, state):
        new = []
        for h in range(HEADS_PER_GROUP):
            m, l, acc = state[h]
            s = s_ref[slot, h]
            m_new = jnp.maximum(m, jnp.max(s, axis=0, keepdims=True))
            a = jnp.exp2(m - m_new)
            p = jnp.exp2(s - m_new)
            l = a * l + jnp.sum(p, axis=0, keepdims=True)
            pb = p.astype(BF16)
            acc = a * acc
            for i in range(chunk):
                acc = acc + _dot(head_rows(vt_ref[c * chunk + i], h), pb[i * blk:(i + 1) * blk, :])
            new.append((m_new, l, acc))
        return tuple(new)

    def past_pair(i, state):
        put_scores(2 * i + 1, 1)
        state = consume(2 * i, 0, state)
        put_scores(jnp.minimum(2 * i + 2, last_chunk), 0)
        return consume(2 * i + 1, 1, state)

    put_scores(0, 0)
    fin = lax.fori_loop(0, (n_chunks + 1) // 2, past_pair, tuple(init))
    o_t = jnp.concatenate([acc / l for (_, l, acc) in fin], axis=0)
    o_ref[...] = o_t.T.astype(o_ref.dtype)


def _attn_prompt(q, k, v):
    bsz, s, w = q.shape
    n_blocks = s // MOBA_BLOCK
    chunk = 4
    assert n_blocks % (2 * chunk) == 0 and n_blocks % 16 == 0 and n_blocks <= LANES
    groups = w // LANES
    return pl.pallas_call(
        functools.partial(_attn_prompt_kernel, n_blocks=n_blocks, chunk=chunk),
        grid=(bsz, groups, n_blocks),
        in_specs=[
            pl.BlockSpec((None, MOBA_BLOCK, LANES), lambda b, g, t: (b, t, g)),
            pl.BlockSpec((None, s, LANES), lambda b, g, t: (b, 0, g)),
            pl.BlockSpec((None, s, LANES), lambda b, g, t: (b, 0, g)),
        ],
        out_specs=pl.BlockSpec((None, MOBA_BLOCK, LANES), lambda b, g, t: (b, t, g)),
        out_shape=jax.ShapeDtypeStruct((bsz, s, w), BF16),
        scratch_shapes=[
            pltpu.VMEM((s, 2 * LANES), BF16),
            pltpu.VMEM((n_blocks, LANES, MOBA_BLOCK), BF16),
            pltpu.VMEM((n_blocks, LANES), F32),
            pltpu.VMEM((2, HEADS_PER_GROUP, chunk * MOBA_BLOCK, MOBA_BLOCK), F32),
        ],
        compiler_params=pltpu.CompilerParams(
            dimension_semantics=("arbitrary", "arbitrary", "arbitrary"),
            vmem_limit_bytes=VMEM_LIMIT),
        name="attn_prompt",
    )(q, k, v)


def _attn_sample_kernel(pt_ref, q_ref, kn_ref, vn_ref, *rest, n_pages, n_new):
    del pt_ref
    k_refs = rest[:n_pages]
    v_refs = rest[n_pages:2 * n_pages]
    o_ref = rest[2 * n_pages]
    pages_per_blk = MOBA_BLOCK // PAGE_SIZE
    n_blk = n_pages // pages_per_blk

    q = q_ref[...]
    r = lax.broadcasted_iota(jnp.int32, q.shape, 0)
    lane = lax.broadcasted_iota(jnp.int32, q.shape, 1)
    diag = (lane // HEAD_DIM) == (r // n_new)
    qbd = jnp.where(diag, q, jnp.zeros_like(q))
    qf = qbd.astype(F32)

    s_page = [_dot(qbd, k_refs[p][...].astype(BF16)) for p in range(n_pages)]
    sb = []
    for j in range(n_blk):
        tot = s_page[j * pages_per_blk]
        for i in range(1, pages_per_blk):
            tot = tot + s_page[j * pages_per_blk + i]
        sb.append(jnp.sum(tot, axis=-1, keepdims=True) * (1.0 / MOBA_BLOCK))

    sel = []
    for j in range(n_blk):
        rank = jnp.zeros(sb[j].shape, jnp.int32)
        for i in range(n_blk):
            if i == j:
                continue
            beats = (sb[i] >= sb[j]) if i < j else (sb[i] > sb[j])
            rank = rank + beats.astype(jnp.int32)
        sel.append(rank < MOBA_TOPK)

    tok = r[:, 0:1] % n_new
    s_past = [jnp.where(sel[p // pages_per_blk], s_page[p], MASK_VALUE) for p in range(n_pages)]
    kn = kn_ref[...]
    vn = vn_ref[...]
    s_new = [jnp.where(tok >= i, jnp.sum(qf * kn[i:i + 1, :], axis=-1, keepdims=True), MASK_VALUE)
             for i in range(n_new)]
    m = s_new[0]
    for s in s_new[1:]:
        m = jnp.maximum(m, s)
    for s in s_past:
        m = jnp.maximum(m, jnp.max(s, axis=-1, keepdims=True))

    l = jnp.zeros(m.shape, F32)
    acc = jnp.zeros(q.shape, F32)
    for i in range(n_new):
        p = jnp.exp2(s_new[i] - m)
        l = l + p
        acc = acc + p * vn[i:i + 1, :]
    for pg in range(n_pages):
        p = jnp.exp2(s_past[pg] - m)
        l = l + jnp.sum(p, axis=-1, keepdims=True)
        acc = acc + _dot_nt(p.astype(BF16), v_refs[pg][...].astype(BF16))
    o = jnp.where(diag, acc / l, 0.0)
    while o.shape[0] > 8:
        half = o.shape[0] // 2
        o = o[:half] + o[half:]
    shift = 4
    while shift >= n_new:
        o = o + pltpu.roll(o, shift, axis=0)
        shift //= 2
    o_ref[...] = o.astype(o_ref.dtype)


def _attn_sample(q, k_new, v_new, cache_k, cache_v, page_table):
    n, t, w = q.shape
    n_pages = page_table.shape[1]
    assert (n_pages * PAGE_SIZE) % MOBA_BLOCK == 0 and MOBA_BLOCK % PAGE_SIZE == 0
    assert t in (1, 2, 4, 8) and n_pages * PAGE_SIZE // MOBA_BLOCK >= MOBA_TOPK
    n_pool = cache_k.shape[0]
    ck = cache_k.transpose(0, 2, 3, 1).reshape(n_pool, w, PAGE_SIZE)
    cv = cache_v.transpose(0, 2, 3, 1).reshape(n_pool, w, PAGE_SIZE)
    q_rep = jnp.tile(q, (1, N_HEADS, 1))
    pad = ((0, 0), (0, 8 - t), (0, 0))
    kn = jnp.pad(k_new, pad)
    vn = jnp.pad(v_new, pad)

    def page_spec(p):
        return pl.BlockSpec((None, w, PAGE_SIZE), lambda i, pt: (pt[i * n_pages + p], 0, 0))

    grid_spec = pltpu.PrefetchScalarGridSpec(
        num_scalar_prefetch=1,
        grid=(n,),
        in_specs=[
            pl.BlockSpec((None, N_HEADS * t, w), lambda i, pt: (i, 0, 0)),
            pl.BlockSpec((None, 8, w), lambda i, pt: (i, 0, 0)),
            pl.BlockSpec((None, 8, w), lambda i, pt: (i, 0, 0)),
        ] + [page_spec(p) for p in range(n_pages)] * 2,
        out_specs=pl.BlockSpec((None, 8, w), lambda i, pt: (i, 0, 0)),
    )
    out = pl.pallas_call(
        functools.partial(_attn_sample_kernel, n_pages=n_pages, n_new=t),
        grid_spec=grid_spec,
        out_shape=jax.ShapeDtypeStruct((n, 8, w), BF16),
        compiler_params=pltpu.CompilerParams(
            dimension_semantics=("arbitrary",), vmem_limit_bytes=VMEM_LIMIT),
        name="attn_sample",
    )(page_table.reshape(-1), q_rep, kn, vn, *([ck] * n_pages), *([cv] * n_pages))
    return out[:, :t]


def _merge_ffn_kernel(x_ref, ao_ref, co_ref, ga_ref, gc_ref, wpa_ref, wpc_ref, wo_ref,
                      g1_ref, g2_ref, g3_ref, wg_ref, wu_ref, wd_ref, y_ref, a_ref, *, ff_chunk):
    m = ga_ref[...] * _dot(ao_ref[...], wpa_ref[...]) + gc_ref[...] * _dot(co_ref[...], wpc_ref[...])
    h = x_ref[...] + _rms(_dot(m.astype(BF16), wo_ref[...])) * g1_ref[...]
    hn = (_rms(h) * g2_ref[...]).astype(BF16)
    d_ff = wg_ref.shape[1]
    for c in range(0, d_ff, ff_chunk):
        g = _dot(hn, wg_ref[:, c:c + ff_chunk])
        u = _dot(hn, wu_ref[:, c:c + ff_chunk])
        a_ref[:, c:c + ff_chunk] = (g * jax.nn.sigmoid(g) * u).astype(BF16)
    f = _dot(a_ref[...], wd_ref[...])
    y_ref[...] = h + _rms(f) * g3_ref[...]


def _merge_ffn(x, ao, co, ga, gc, wpa, wpc, wo, g1, g2, g3, wg, wu, wd, tm):
    rows, d = x.shape
    cw = ao.shape[1]
    d_ff = wg.shape[1]
    ff_chunk = 256
    assert d_ff % ff_chunk == 0 and rows % tm == 0
    row_blk = lambda width: pl.BlockSpec((tm, width), lambda i: (i, 0))
    return pl.pallas_call(
        functools.partial(_merge_ffn_kernel, ff_chunk=ff_chunk),
        grid=(rows // tm,),
        in_specs=[row_blk(d), row_blk(cw), row_blk(cw), row_blk(d), row_blk(d),
                  _resident(wpa.shape), _resident(wpc.shape), _resident(wo.shape),
                  _resident((1, d)), _resident((1, d)), _resident((1, d)),
                  _resident(wg.shape), _resident(wu.shape), _resident(wd.shape)],
        out_specs=row_blk(d),
        out_shape=jax.ShapeDtypeStruct((rows, d), F32),
        scratch_shapes=[pltpu.VMEM((tm, d_ff), BF16)],
        compiler_params=pltpu.CompilerParams(
            dimension_semantics=("arbitrary",), vmem_limit_bytes=VMEM_LIMIT),
        name="merge_ffn",
    )(x, ao, co, ga, gc, wpa, wpc, wo, g1.reshape(1, d), g2.reshape(1, d), g3.reshape(1, d),
      wg, wu, wd)


def kernel(x_prompt, x_sample, cache_k, cache_v, state_conv, page_table, g_attn_pre, w_in, w_conv, w_proj_attn, w_proj_conv, w_out, g_attn_post, g_ffn_pre, w_gate, w_up, w_down, g_ffn_post):
    depth = w_in.shape[0]
    bsz, s, d = x_prompt.shape
    n, t, _ = x_sample.shape
    cw = ATTN_WIDTH
    assert t >= CONV_WIDTH - 1 and s >= CONV_WIDTH - 1
    xp, xs = x_prompt, x_sample
    outs = [[] for _ in range(6)]
    for l in range(depth):
        w_in_b = w_in[l].astype(BF16)
        tail = (w_proj_attn[l].astype(BF16), w_proj_conv[l].astype(BF16), w_out[l].astype(BF16),
                g_attn_post[l], g_ffn_pre[l], g_ffn_post[l],
                w_gate[l].astype(BF16), w_up[l].astype(BF16), w_down[l].astype(BF16))

        prev0 = jnp.zeros((bsz, CONV_WIDTH - 1, cw), F32)
        q, k, v, co, ga, gc, u_tail = _proj_prompt(xp, g_attn_pre[l], w_in_b, w_conv[l], prev0, 512)
        ao = _attn_prompt(q, k, v)
        flat = lambda a: a.reshape(bsz * s, a.shape[-1])
        xp = _merge_ffn(flat(xp), flat(ao), flat(co), flat(ga), flat(gc), *tail, 256)
        xp = xp.reshape(bsz, s, d)
        outs[0].append(k.reshape(bsz, s, N_HEADS, HEAD_DIM))
        outs[1].append(v.reshape(bsz, s, N_HEADS, HEAD_DIM))
        outs[2].append(u_tail[:, 8 - (CONV_WIDTH - 1):])

        q, k, v, co, ga, gc, u = _proj_sample(xs, g_attn_pre[l], w_in_b, w_conv[l], state_conv[l])
        ao = _attn_sample(q.reshape(n, t, cw), k.reshape(n, t, cw), v.reshape(n, t, cw),
                          cache_k[l], cache_v[l], page_table)
        xs = _merge_ffn(xs.reshape(n * t, d), ao.reshape(n * t, cw), co, ga, gc, *tail, 256)
        xs = xs.reshape(n, t, d)
        outs[3].append(k.reshape(n, t, N_HEADS, HEAD_DIM))
        outs[4].append(v.reshape(n, t, N_HEADS, HEAD_DIM))
        outs[5].append(u.reshape(n, t, cw)[:, t - (CONV_WIDTH - 1):])

    kp, vp, cp, ks, vs, cs = (jnp.stack(o) for o in outs)
    return (xp, xs, kp, vp, cp, ks, vs, cs)
```

```python
import functools

import jax
import jax.numpy as jnp
from jax import lax
from jax.experimental import pallas as pl
from jax.experimental.pallas import tpu as pltpu

N_HEADS = 8
HEAD_DIM = 64
ATTN_WIDTH = N_HEADS * HEAD_DIM
MOBA_BLOCK = 256
MOBA_TOPK = 3
CONV_WIDTH = 3
PAGE_SIZE = 128
RMS_EPS = 1e-6

LANES = 128
HEADS_PER_GROUP = LANES // HEAD_DIM
MASK_VALUE = -(2.0 ** 100)
Q_SCALE = HEAD_DIM ** -0.5 * 1.4426950408889634
VMEM_LIMIT = 52 * 1024 * 1024

F32 = jnp.float32
BF16 = jnp.bfloat16


def _dot(a, b):
    return jnp.dot(a, b, preferred_element_type=F32)


def _dot_nt(a, b):
    return lax.dot_general(a, b, (((1,), (1,)), ((), ())), preferred_element_type=F32)


def _rms(x):
    return x * lax.rsqrt(jnp.mean(x * x, axis=-1, keepdims=True) + RMS_EPS)


def _resident(shape):
    return pl.BlockSpec(shape, lambda *_: (0,) * len(shape), pipeline_mode=pl.Buffered(1))


def _proj_kernel(x_ref, g_ref, w_ref, wc_ref, b1_ref, b2_ref,
                 q_ref, k_ref, v_ref, co_ref, ga_ref, gc_ref, u_ref,
                 carry_ref, *, seq_rows, d_model):
    tm = x_ref.shape[0]
    cw = ATTN_WIDTH
    xn = (_rms(x_ref[...]) * g_ref[...]).astype(BF16)

    def proj(lo, width):
        return _dot(xn, w_ref[:, lo:lo + width])

    q_ref[...] = (proj(0, cw) * Q_SCALE).astype(BF16)
    k_ref[...] = proj(cw, cw)
    v_ref[...] = proj(2 * cw, cw)
    cc = proj(4 * cw, cw)
    cx = proj(5 * cw, cw)
    u = cc * cx
    row = lax.broadcasted_iota(jnp.int32, u.shape, 0)
    s1 = pltpu.roll(u, 1, axis=0)
    s2 = pltpu.roll(u, 2, axis=0)
    if seq_rows is None:
        @pl.when(pl.program_id(1) == 0)
        def _():
            carry_ref[...] = b1_ref[0]
        c = carry_ref[...]
        prev1 = jnp.broadcast_to(c[7:8, :], u.shape)
        prev2 = jnp.broadcast_to(c[6:7, :], u.shape)
        s1 = jnp.where(row == 0, prev1, s1)
        s2 = jnp.where(row == 0, prev2, jnp.where(row == 1, prev1, s2))
        carry_ref[...] = u[tm - 8:tm, :]
        u_ref[0] = u[tm - 8:tm, :]
    else:
        t = row % seq_rows
        s1 = jnp.where(t >= 1, s1, b1_ref[...])
        s2 = jnp.where(t >= 2, s2, b2_ref[...])
        u_ref[...] = u
    wc = wc_ref[...]
    y = wc[0:1, :] * s2 + wc[1:2, :] * s1 + wc[2:3, :] * u
    co_ref[...] = (proj(3 * cw, cw) * y).astype(BF16)
    ga_ref[...] = jax.nn.sigmoid(proj(6 * cw, d_model))
    gc_ref[...] = jax.nn.sigmoid(proj(6 * cw + d_model, d_model))


def _proj_prompt(x, g, w_in, w_conv, prev, tm):
    bsz, s, d = x.shape
    cw = ATTN_WIDTH
    hist = jnp.zeros((bsz, 8, cw), F32).at[:, 8 - (CONV_WIDTH - 1):].set(prev)
    row_blk = lambda width: pl.BlockSpec((None, tm, width), lambda b, i: (b, i, 0))
    outs = pl.pallas_call(
        functools.partial(_proj_kernel, seq_rows=None, d_model=d),
        grid=(bsz, s // tm),
        in_specs=[
            row_blk(d),
            _resident((1, d)),
            _resident(w_in.shape),
            _resident(w_conv.shape),
            pl.BlockSpec((1, 8, cw), lambda b, i: (b, 0, 0)),
            pl.BlockSpec((1, 8, cw), lambda b, i: (b, 0, 0)),
        ],
        out_specs=[row_blk(cw), row_blk(cw), row_blk(cw), row_blk(cw), row_blk(d), row_blk(d),
                   pl.BlockSpec((1, 8, cw), lambda b, i: (b, 0, 0))],
        out_shape=[
            jax.ShapeDtypeStruct((bsz, s, cw), BF16),
            jax.ShapeDtypeStruct((bsz, s, cw), F32),
            jax.ShapeDtypeStruct((bsz, s, cw), F32),
            jax.ShapeDtypeStruct((bsz, s, cw), BF16),
            jax.ShapeDtypeStruct((bsz, s, d), F32),
            jax.ShapeDtypeStruct((bsz, s, d), F32),
            jax.ShapeDtypeStruct((bsz, 8, cw), F32),
        ],
        scratch_shapes=[pltpu.VMEM((8, cw), F32)],
        compiler_params=pltpu.CompilerParams(
            dimension_semantics=("arbitrary", "arbitrary"), vmem_limit_bytes=VMEM_LIMIT),
        name="proj_prompt",
    )(x, g.reshape(1, d), w_in, w_conv, hist, hist)
    return outs


def _proj_sample(x, g, w_in, w_conv, state):
    n, t, d = x.shape
    cw = ATTN_WIDTH
    rows = n * t
    b1 = jnp.zeros((n, t, cw), F32).at[:, 0].set(state[:, 1]).reshape(rows, cw)
    b2 = jnp.zeros((n, t, cw), F32).at[:, 0].set(state[:, 0]).at[:, 1].set(state[:, 1])
    b2 = b2.reshape(rows, cw)
    full = lambda width: pl.BlockSpec((rows, width), lambda i: (0, 0))
    outs = pl.pallas_call(
        functools.partial(_proj_kernel, seq_rows=t, d_model=d),
        grid=(1,),
        in_specs=[full(d), _resident((1, d)), _resident(w_in.shape), _resident(w_conv.shape),
                  full(cw), full(cw)],
        out_specs=[full(cw), full(cw), full(cw), full(cw), full(d), full(d), full(cw)],
        out_shape=[
            jax.ShapeDtypeStruct((rows, cw), BF16),
            jax.ShapeDtypeStruct((rows, cw), F32),
            jax.ShapeDtypeStruct((rows, cw), F32),
            jax.ShapeDtypeStruct((rows, cw), BF16),
            jax.ShapeDtypeStruct((rows, d), F32),
            jax.ShapeDtypeStruct((rows, d), F32),
            jax.ShapeDtypeStruct((rows, cw), F32),
        ],
        scratch_shapes=[pltpu.VMEM((8, cw), F32)],
        compiler_params=pltpu.CompilerParams(
            dimension_semantics=("arbitrary",), vmem_limit_bytes=VMEM_LIMIT),
        name="proj_sample",
    )(x.reshape(rows, d), g.reshape(1, d), w_in, w_conv, b1, b2)
    return outs


def _select_blocks(scores, n_valid):
    blk_id = lax.broadcasted_iota(jnp.int32, scores.shape, 0)
    blk_f = blk_id.astype(F32)
    cand = jnp.where(blk_id < n_valid, scores, -jnp.inf)
    sel = jnp.zeros(scores.shape, jnp.bool_)
    for _ in range(MOBA_TOPK):
        mx = jnp.max(cand, axis=0, keepdims=True)
        idx = jnp.min(jnp.where(cand == mx, blk_f, float(scores.shape[0])), axis=0, keepdims=True)
        pick = (blk_f == idx) & (mx > -jnp.inf)
        sel = sel | pick
        cand = jnp.where(pick, -jnp.inf, cand)
    return sel


def _attn_prompt_kernel(q_ref, k_ref, v_ref, o_ref, kaug_ref, vt_ref, kmean_ref, s_ref, *,
                        n_blocks, chunk):
    t = pl.program_id(2)
    blk = MOBA_BLOCK

    @pl.when(t == 0)
    def _init():
        lane_k = lax.broadcasted_iota(jnp.int32, (blk, LANES), 1)

        def fill(j, c):
            rows = pl.ds(pl.multiple_of(j * blk, blk), blk)
            kb = k_ref[rows, :]
            kaug_ref[rows, 0:LANES] = kb.astype(BF16)
            kaug_ref[rows, LANES:2 * LANES] = (lane_k == j).astype(BF16)
            vt_ref[j] = v_ref[rows, :].T.astype(BF16)
            kmean_ref[pl.ds(j, 1), :] = jnp.sum(kb, axis=0, keepdims=True) * (1.0 / blk)
            return c

        lax.fori_loop(0, n_blocks, fill, 0)

    q_t = q_ref[...].astype(F32).T
    d_row = lax.broadcasted_iota(jnp.int32, q_t.shape, 0)
    kmean = kmean_ref[...].astype(BF16)
    key_i = lax.broadcasted_iota(jnp.int32, (blk, blk), 0)
    qry_i = lax.broadcasted_iota(jnp.int32, (blk, blk), 1)
    causal = key_i <= qry_i
    own = pl.ds(pl.multiple_of(t * blk, blk), blk)
    k_own = kaug_ref[own, 0:LANES]
    v_own = vt_ref[t]
    pad_rows = jnp.zeros((LANES - n_blocks, blk), BF16)

    def head_rows(a, h):
        return a[h * HEAD_DIM:(h + 1) * HEAD_DIM, :]

    heads = range(HEADS_PER_GROUP)
    last_chunk = n_blocks // chunk - 1

    def put_scores(c, slot):
        rows = pl.ds(pl.multiple_of(c * (chunk * blk), chunk * blk), chunk * blk)
        kc = kaug_ref[rows, :]
        for h in heads:
            s_ref[slot, h] = _dot(kc, q_aug[h])

    def consume(c, slot, state):
        new = []
        for h in heads:
            m, l, acc = state[h]
            s = s_ref[slot, h]
            m_new = jnp.maximum(m, jnp.max(s, axis=0, keepdims=True))
            a = jnp.exp2(m - m_new)
            p = jnp.exp2(s - m_new)
            l = a * l + jnp.sum(p, axis=0, keepdims=True)
            pb = p.astype(BF16)
            acc = a * acc
            for i in range(chunk):
                acc = acc + _dot(head_rows(vt_ref[c * chunk + i], h), pb[i * blk:(i + 1) * blk, :])
            new.append((m_new, l, acc))
        return tuple(new)

    qh = []
    for h in heads:
        in_head = (d_row >= h * HEAD_DIM) & (d_row < (h + 1) * HEAD_DIM)
        qh.append(jnp.where(in_head, q_t, 0.0).astype(BF16))
    blk_scores = [_dot(kmean, qh[h]) for h in heads]
    own_scores = [_dot(k_own, qh[h]) for h in heads]
    q_aug = []
    for h in heads:
        pen = jnp.where(_select_blocks(blk_scores[h], t), 0.0, MASK_VALUE).astype(BF16)
        q_aug.append(jnp.concatenate([qh[h], pen, pad_rows], axis=0))
    put_scores(0, 0)
    state = []
    for h in heads:
        s = jnp.where(causal, own_scores[h], MASK_VALUE)
        m = jnp.max(s, axis=0, keepdims=True)
        p = jnp.exp2(s - m)
        state.append((m, jnp.sum(p, axis=0, keepdims=True),
                      _dot(head_rows(v_own, h), p.astype(BF16))))

    n_chunks = (t + chunk - 1) // chunk

    def step_pair(j, state):
        put_scores(2 * j + 1, 1)
        state = consume(2 * j, 0, state)
        put_scores(jnp.minimum(2 * j + 2, last_chunk), 0)
        return consume(2 * j + 1, 1, state)

    state = lax.fori_loop(0, n_chunks // 2, step_pair, tuple(state))
    fin = lax.cond(n_chunks % 2 == 1, lambda st: consume(n_chunks - 1, 0, st), lambda st: st, state)
    o_t = jnp.concatenate([acc * (1.0 / l) for (_, l, acc) in fin], axis=0)
    o_ref[...] = o_t.T.astype(o_ref.dtype)


def _attn_prompt(q, k, v):
    bsz, s, w = q.shape
    n_blocks = s // MOBA_BLOCK
    chunk = 4
    assert n_blocks % (2 * chunk) == 0 and n_blocks % 16 == 0 and n_blocks <= LANES
    groups = w // LANES
    return pl.pallas_call(
        functools.partial(_attn_prompt_kernel, n_blocks=n_blocks, chunk=chunk),
        grid=(bsz, groups, n_blocks),
        in_specs=[
            pl.BlockSpec((None, MOBA_BLOCK, LANES), lambda b, g, t: (b, t, g)),
            pl.BlockSpec((None, s, LANES), lambda b, g, t: (b, 0, g)),
            pl.BlockSpec((None, s, LANES), lambda b, g, t: (b, 0, g)),
        ],
        out_specs=pl.BlockSpec((None, MOBA_BLOCK, LANES), lambda b, g, t: (b, t, g)),
        out_shape=jax.ShapeDtypeStruct((bsz, s, w), BF16),
        scratch_shapes=[
            pltpu.VMEM((s, 2 * LANES), BF16),
            pltpu.VMEM((n_blocks, LANES, MOBA_BLOCK), BF16),
            pltpu.VMEM((n_blocks, LANES), F32),
            pltpu.VMEM((2, HEADS_PER_GROUP, chunk * MOBA_BLOCK, MOBA_BLOCK), F32),
        ],
        compiler_params=pltpu.CompilerParams(
            dimension_semantics=("arbitrary", "arbitrary", "arbitrary"),
            vmem_limit_bytes=VMEM_LIMIT),
        name="attn_prompt",
    )(q, k, v)


def _attn_sample_kernel(pt_ref, q_ref, kn_ref, vn_ref, *rest, n_pages, n_new):
    del pt_ref
    k_refs = rest[:n_pages]
    v_refs = rest[n_pages:2 * n_pages]
    o_ref = rest[2 * n_pages]
    pages_per_blk = MOBA_BLOCK // PAGE_SIZE
    n_blk = n_pages // pages_per_blk

    q = q_ref[...]
    r = lax.broadcasted_iota(jnp.int32, q.shape, 0)
    lane = lax.broadcasted_iota(jnp.int32, q.shape, 1)
    diag = (lane // HEAD_DIM) == (r // n_new)
    qbd = jnp.where(diag, q, jnp.zeros_like(q))
    qf = qbd.astype(F32)

    s_page = [_dot(qbd, k_refs[p][...].astype(BF16)) for p in range(n_pages)]
    sb = []
    for j in range(n_blk):
        tot = s_page[j * pages_per_blk]
        for i in range(1, pages_per_blk):
            tot = tot + s_page[j * pages_per_blk + i]
        sb.append(jnp.sum(tot, axis=-1, keepdims=True) * (1.0 / MOBA_BLOCK))

    sel = []
    for j in range(n_blk):
        rank = jnp.zeros(sb[j].shape, jnp.int32)
        for i in range(n_blk):
            if i == j:
                continue
            beats = (sb[i] >= sb[j]) if i < j else (sb[i] > sb[j])
            rank = rank + beats.astype(jnp.int32)
        sel.append(rank < MOBA_TOPK)

    tok = r[:, 0:1] % n_new
    s_past = [jnp.where(sel[p // pages_per_blk], s_page[p], MASK_VALUE) for p in range(n_pages)]
    kn = kn_ref[...]
    vn = vn_ref[...]
    s_new = [jnp.where(tok >= i, jnp.sum(qf * kn[i:i + 1, :], axis=-1, keepdims=True), MASK_VALUE)
             for i in range(n_new)]
    m = s_new[0]
    for s in s_new[1:]:
        m = jnp.maximum(m, s)
    for s in s_past:
        m = jnp.maximum(m, jnp.max(s, axis=-1, keepdims=True))

    l = jnp.zeros(m.shape, F32)
    acc = jnp.zeros(q.shape, F32)
    for i in range(n_new):
        p = jnp.exp2(s_new[i] - m)
        l = l + p
        acc = acc + p * vn[i:i + 1, :]
    for pg in range(n_pages):
        p = jnp.exp2(s_past[pg] - m)
        l = l + jnp.sum(p, axis=-1, keepdims=True)
        acc = acc + _dot_nt(p.astype(BF16), v_refs[pg][...].astype(BF16))
    o = jnp.where(diag, acc / l, 0.0)
    while o.shape[0] > 8:
        half = o.shape[0] // 2
        o = o[:half] + o[half:]
    shift = 4
    while shift >= n_new:
        o = o + pltpu.roll(o, shift, axis=0)
        shift //= 2
    o_ref[...] = o.astype(o_ref.dtype)


def _attn_sample(q, k_new, v_new, cache_k, cache_v, page_table):
    n, t, w = q.shape
    n_pages = page_table.shape[1]
    assert (n_pages * PAGE_SIZE) % MOBA_BLOCK == 0 and MOBA_BLOCK % PAGE_SIZE == 0
    assert t in (1, 2, 4, 8) and n_pages * PAGE_SIZE // MOBA_BLOCK >= MOBA_TOPK
    n_pool = cache_k.shape[0]
    ck = cache_k.transpose(0, 2, 3, 1).reshape(n_pool, w, PAGE_SIZE)
    cv = cache_v.transpose(0, 2, 3, 1).reshape(n_pool, w, PAGE_SIZE)
    q_rep = jnp.tile(q, (1, N_HEADS, 1))
    pad = ((0, 0), (0, 8 - t), (0, 0))
    kn = jnp.pad(k_new, pad)
    vn = jnp.pad(v_new, pad)

    def page_spec(p):
        return pl.BlockSpec((None, w, PAGE_SIZE), lambda i, pt: (pt[i * n_pages + p], 0, 0))

    grid_spec = pltpu.PrefetchScalarGridSpec(
        num_scalar_prefetch=1,
        grid=(n,),
        in_specs=[
            pl.BlockSpec((None, N_HEADS * t, w), lambda i, pt: (i, 0, 0)),
            pl.BlockSpec((None, 8, w), lambda i, pt: (i, 0, 0)),
            pl.BlockSpec((None, 8, w), lambda i, pt: (i, 0, 0)),
        ] + [page_spec(p) for p in range(n_pages)] * 2,
        out_specs=pl.BlockSpec((None, 8, w), lambda i, pt: (i, 0, 0)),
    )
    out = pl.pallas_call(
        functools.partial(_attn_sample_kernel, n_pages=n_pages, n_new=t),
        grid_spec=grid_spec,
        out_shape=jax.ShapeDtypeStruct((n, 8, w), BF16),
        compiler_params=pltpu.CompilerParams(
            dimension_semantics=("arbitrary",), vmem_limit_bytes=VMEM_LIMIT),
        name="attn_sample",
    )(page_table.reshape(-1), q_rep, kn, vn, *([ck] * n_pages), *([cv] * n_pages))
    return out[:, :t]


def _merge_ffn_kernel(x_ref, ao_ref, co_ref, ga_ref, gc_ref, wpa_ref, wpc_ref, wo_ref,
                      g1_ref, g2_ref, g3_ref, wg_ref, wu_ref, wd_ref, y_ref, a_ref, *, ff_chunk):
    m = ga_ref[...] * _dot(ao_ref[...], wpa_ref[...]) + gc_ref[...] * _dot(co_ref[...], wpc_ref[...])
    h = x_ref[...] + _rms(_dot(m.astype(BF16), wo_ref[...])) * g1_ref[...]
    hn = (_rms(h) * g2_ref[...]).astype(BF16)
    d_ff = wg_ref.shape[1]
    for c in range(0, d_ff, ff_chunk):
        g = _dot(hn, wg_ref[:, c:c + ff_chunk])
        u = _dot(hn, wu_ref[:, c:c + ff_chunk])
        a_ref[:, c:c + ff_chunk] = (g * jax.nn.sigmoid(g) * u).astype(BF16)
    f = _dot(a_ref[...], wd_ref[...])
    y_ref[...] = h + _rms(f) * g3_ref[...]


def _merge_ffn(x, ao, co, ga, gc, wpa, wpc, wo, g1, g2, g3, wg, wu, wd, tm):
    rows, d = x.shape
    cw = ao.shape[1]
    d_ff = wg.shape[1]
    ff_chunk = 256
    assert d_ff % ff_chunk == 0 and rows % tm == 0
    row_blk = lambda width: pl.BlockSpec((tm, width), lambda i: (i, 0))
    return pl.pallas_call(
        functools.partial(_merge_ffn_kernel, ff_chunk=ff_chunk),
        grid=(rows // tm,),
        in_specs=[row_blk(d), row_blk(cw), row_blk(cw), row_blk(d), row_blk(d),
                  _resident(wpa.shape), _resident(wpc.shape), _resident(wo.shape),
                  _resident((1, d)), _resident((1, d)), _resident((1, d)),
                  _resident(wg.shape), _resident(wu.shape), _resident(wd.shape)],
        out_specs=row_blk(d),
        out_shape=jax.ShapeDtypeStruct((rows, d), F32),
        scratch_shapes=[pltpu.VMEM((tm, d_ff), BF16)],
        compiler_params=pltpu.CompilerParams(
            dimension_semantics=("arbitrary",), vmem_limit_bytes=VMEM_LIMIT),
        name="merge_ffn",
    )(x, ao, co, ga, gc, wpa, wpc, wo, g1.reshape(1, d), g2.reshape(1, d), g3.reshape(1, d),
      wg, wu, wd)


def kernel(x_prompt, x_sample, cache_k, cache_v, state_conv, page_table, g_attn_pre, w_in, w_conv, w_proj_attn, w_proj_conv, w_out, g_attn_post, g_ffn_pre, w_gate, w_up, w_down, g_ffn_post):
    depth = w_in.shape[0]
    bsz, s, d = x_prompt.shape
    n, t, _ = x_sample.shape
    cw = ATTN_WIDTH
    assert t >= CONV_WIDTH - 1 and s >= CONV_WIDTH - 1
    xp, xs = x_prompt, x_sample
    outs = [[] for _ in range(6)]
    for l in range(depth):
        w_in_b = w_in[l].astype(BF16)
        tail = (w_proj_attn[l].astype(BF16), w_proj_conv[l].astype(BF16), w_out[l].astype(BF16),
                g_attn_post[l], g_ffn_pre[l], g_ffn_post[l],
                w_gate[l].astype(BF16), w_up[l].astype(BF16), w_down[l].astype(BF16))

        prev0 = jnp.zeros((bsz, CONV_WIDTH - 1, cw), F32)
        q, k, v, co, ga, gc, u_tail = _proj_prompt(xp, g_attn_pre[l], w_in_b, w_conv[l], prev0, 512)
        ao = _attn_prompt(q, k, v)
        flat = lambda a: a.reshape(bsz * s, a.shape[-1])
        xp = _merge_ffn(flat(xp), flat(ao), flat(co), flat(ga), flat(gc), *tail, 512)
        xp = xp.reshape(bsz, s, d)
        outs[0].append(k.reshape(bsz, s, N_HEADS, HEAD_DIM))
        outs[1].append(v.reshape(bsz, s, N_HEADS, HEAD_DIM))
        outs[2].append(u_tail[:, 8 - (CONV_WIDTH - 1):])

        q, k, v, co, ga, gc, u = _proj_sample(xs, g_attn_pre[l], w_in_b, w_conv[l], state_conv[l])
        ao = _attn_sample(q.reshape(n, t, cw), k.reshape(n, t, cw), v.reshape(n, t, cw),
                          cache_k[l], cache_v[l], page_table)
        xs = _merge_ffn(xs.reshape(n * t, d), ao.reshape(n * t, cw), co, ga, gc, *tail, 256)
        xs = xs.reshape(n, t, d)
        outs[3].append(k.reshape(n, t, N_HEADS, HEAD_DIM))
        outs[4].append(v.reshape(n, t, N_HEADS, HEAD_DIM))
        outs[5].append(u.reshape(n, t, cw)[:, t - (CONV_WIDTH - 1):])

    kp, vp, cp, ks, vs, cs = (jnp.stack(o) for o in outs)
    return (xp, xs, kp, vp, cp, ks, vs, cs)
```

```python
import functools

import jax
import jax.numpy as jnp
from jax import lax
from jax.experimental import pallas as pl
from jax.experimental.pallas import tpu as pltpu

N_HEADS = 8
HEAD_DIM = 64
ATTN_WIDTH = N_HEADS * HEAD_DIM
MOBA_BLOCK = 256
MOBA_TOPK = 3
CONV_WIDTH = 3
PAGE_SIZE = 128
RMS_EPS = 1e-6

LANES = 128
HEADS_PER_GROUP = LANES // HEAD_DIM
MASK_VALUE = -(2.0 ** 100)
Q_SCALE = HEAD_DIM ** -0.5 * 1.4426950408889634
VMEM_LIMIT = 52 * 1024 * 1024

F32 = jnp.float32
BF16 = jnp.bfloat16


def _dot(a, b):
    return jnp.dot(a, b, preferred_element_type=F32)


def _dot_nt(a, b):
    return lax.dot_general(a, b, (((1,), (1,)), ((), ())), preferred_element_type=F32)


def _rms(x):
    return x * lax.rsqrt(jnp.mean(x * x, axis=-1, keepdims=True) + RMS_EPS)


def _resident(shape):
    return pl.BlockSpec(shape, lambda *_: (0,) * len(shape), pipeline_mode=pl.Buffered(1))


def _proj_kernel(x_ref, g_ref, w_ref, wc_ref, b1_ref, b2_ref,
                 q_ref, k_ref, v_ref, co_ref, ga_ref, gc_ref, u_ref,
                 carry_ref, *, seq_rows, d_model):
    tm = x_ref.shape[0]
    cw = ATTN_WIDTH
    xn = (_rms(x_ref[...]) * g_ref[...]).astype(BF16)

    def proj(lo, width):
        return _dot(xn, w_ref[:, lo:lo + width])

    q_ref[...] = (proj(0, cw) * Q_SCALE).astype(BF16)
    k_ref[...] = proj(cw, cw)
    v_ref[...] = proj(2 * cw, cw)
    cc = proj(4 * cw, cw)
    cx = proj(5 * cw, cw)
    u = cc * cx
    row = lax.broadcasted_iota(jnp.int32, u.shape, 0)
    s1 = pltpu.roll(u, 1, axis=0)
    s2 = pltpu.roll(u, 2, axis=0)
    if seq_rows is None:
        @pl.when(pl.program_id(1) == 0)
        def _():
            carry_ref[...] = b1_ref[0]
        c = carry_ref[...]
        prev1 = jnp.broadcast_to(c[7:8, :], u.shape)
        prev2 = jnp.broadcast_to(c[6:7, :], u.shape)
        s1 = jnp.where(row == 0, prev1, s1)
        s2 = jnp.where(row == 0, prev2, jnp.where(row == 1, prev1, s2))
        carry_ref[...] = u[tm - 8:tm, :]
        u_ref[0] = u[tm - 8:tm, :]
    else:
        t = row % seq_rows
        s1 = jnp.where(t >= 1, s1, b1_ref[...])
        s2 = jnp.where(t >= 2, s2, b2_ref[...])
        u_ref[...] = u
    wc = wc_ref[...]
    y = wc[0:1, :] * s2 + wc[1:2, :] * s1 + wc[2:3, :] * u
    co_ref[...] = (proj(3 * cw, cw) * y).astype(BF16)
    ga_ref[...] = jax.nn.sigmoid(proj(6 * cw, d_model))
    gc_ref[...] = jax.nn.sigmoid(proj(6 * cw + d_model, d_model))


def _proj_prompt(x, g, w_in, w_conv, prev, tm):
    bsz, s, d = x.shape
    cw = ATTN_WIDTH
    hist = jnp.zeros((bsz, 8, cw), F32).at[:, 8 - (CONV_WIDTH - 1):].set(prev)
    row_blk = lambda width: pl.BlockSpec((None, tm, width), lambda b, i: (b, i, 0))
    outs = pl.pallas_call(
        functools.partial(_proj_kernel, seq_rows=None, d_model=d),
        grid=(bsz, s // tm),
        in_specs=[
            row_blk(d),
            _resident((1, d)),
            _resident(w_in.shape),
            _resident(w_conv.shape),
            pl.BlockSpec((1, 8, cw), lambda b, i: (b, 0, 0)),
            pl.BlockSpec((1, 8, cw), lambda b, i: (b, 0, 0)),
        ],
        out_specs=[row_blk(cw), row_blk(cw), row_blk(cw), row_blk(cw), row_blk(d), row_blk(d),
                   pl.BlockSpec((1, 8, cw), lambda b, i: (b, 0, 0))],
        out_shape=[
            jax.ShapeDtypeStruct((bsz, s, cw), BF16),
            jax.ShapeDtypeStruct((bsz, s, cw), F32),
            jax.ShapeDtypeStruct((bsz, s, cw), F32),
            jax.ShapeDtypeStruct((bsz, s, cw), BF16),
            jax.ShapeDtypeStruct((bsz, s, d), F32),
            jax.ShapeDtypeStruct((bsz, s, d), F32),
            jax.ShapeDtypeStruct((bsz, 8, cw), F32),
        ],
        scratch_shapes=[pltpu.VMEM((8, cw), F32)],
        compiler_params=pltpu.CompilerParams(
            dimension_semantics=("arbitrary", "arbitrary"), vmem_limit_bytes=VMEM_LIMIT),
        name="proj_prompt",
    )(x, g.reshape(1, d), w_in, w_conv, hist, hist)
    return outs


def _proj_sample(x, g, w_in, w_conv, state):
    n, t, d = x.shape
    cw = ATTN_WIDTH
    rows = n * t
    b1 = jnp.zeros((n, t, cw), F32).at[:, 0].set(state[:, 1]).reshape(rows, cw)
    b2 = jnp.zeros((n, t, cw), F32).at[:, 0].set(state[:, 0]).at[:, 1].set(state[:, 1])
    b2 = b2.reshape(rows, cw)
    full = lambda width: pl.BlockSpec((rows, width), lambda i: (0, 0))
    outs = pl.pallas_call(
        functools.partial(_proj_kernel, seq_rows=t, d_model=d),
        grid=(1,),
        in_specs=[full(d), _resident((1, d)), _resident(w_in.shape), _resident(w_conv.shape),
                  full(cw), full(cw)],
        out_specs=[full(cw), full(cw), full(cw), full(cw), full(d), full(d), full(cw)],
        out_shape=[
            jax.ShapeDtypeStruct((rows, cw), BF16),
            jax.ShapeDtypeStruct((rows, cw), F32),
            jax.ShapeDtypeStruct((rows, cw), F32),
            jax.ShapeDtypeStruct((rows, cw), BF16),
            jax.ShapeDtypeStruct((rows, d), F32),
            jax.ShapeDtypeStruct((rows, d), F32),
            jax.ShapeDtypeStruct((rows, cw), F32),
        ],
        scratch_shapes=[pltpu.VMEM((8, cw), F32)],
        compiler_params=pltpu.CompilerParams(
            dimension_semantics=("arbitrary",), vmem_limit_bytes=VMEM_LIMIT),
        name="proj_sample",
    )(x.reshape(rows, d), g.reshape(1, d), w_in, w_conv, b1, b2)
    return outs


V_ROWS = HEAD_DIM + 16


def _select_blocks(scores, n_valid):
    blk_id = lax.broadcasted_iota(jnp.int32, scores.shape, 0)
    blk_f = blk_id.astype(F32)
    cand = jnp.where(blk_id < n_valid, scores, -jnp.inf)
    sel = jnp.zeros(scores.shape, jnp.bool_)
    for _ in range(MOBA_TOPK):
        mx = jnp.max(cand, axis=0, keepdims=True)
        idx = jnp.min(jnp.where(cand == mx, blk_f, float(scores.shape[0])), axis=0, keepdims=True)
        pick = (blk_f == idx) & (mx > -jnp.inf)
        sel = sel | pick
        cand = jnp.where(pick, -jnp.inf, cand)
    return sel


def _attn_prompt_kernel(q_ref, qn_ref, k_ref, v_ref, o_ref, kaug_ref, vt_ref, kmean_ref, qaug_ref,
                        s_ref, *, n_blocks, chunk):
    t = pl.program_id(2)
    blk = MOBA_BLOCK
    heads = range(HEADS_PER_GROUP)

    def put_query(q_rows_ref, n_valid, slot):
        q_t = q_rows_ref[...].astype(F32).T
        d_row = lax.broadcasted_iota(jnp.int32, q_t.shape, 0)
        kmean = kmean_ref[...].astype(BF16)
        pad_rows = jnp.zeros((LANES - n_blocks, blk), BF16)
        for h in heads:
            in_head = (d_row >= h * HEAD_DIM) & (d_row < (h + 1) * HEAD_DIM)
            qh = jnp.where(in_head, q_t, 0.0).astype(BF16)
            pen = jnp.where(_select_blocks(_dot(kmean, qh), n_valid), 0.0, MASK_VALUE).astype(BF16)
            qaug_ref[slot, h] = jnp.concatenate([qh, pen, pad_rows], axis=0)

    @pl.when(t == 0)
    def _init():
        lane_k = lax.broadcasted_iota(jnp.int32, (blk, LANES), 1)
        pad_i = lax.broadcasted_iota(jnp.int32, (V_ROWS - HEAD_DIM, blk), 0)
        ones_rows = (pad_i == 0).astype(BF16)

        def fill(j, c):
            rows = pl.ds(pl.multiple_of(j * blk, blk), blk)
            kb = k_ref[rows, :]
            kaug_ref[rows, 0:LANES] = kb.astype(BF16)
            kaug_ref[rows, LANES:2 * LANES] = (lane_k == j).astype(BF16)
            v_t = v_ref[rows, :].T
            for h in heads:
                vt_ref[j, h, 0:HEAD_DIM] = v_t[h * HEAD_DIM:(h + 1) * HEAD_DIM, :].astype(BF16)
                vt_ref[j, h, HEAD_DIM:V_ROWS] = ones_rows
            kmean_ref[pl.ds(j, 1), :] = jnp.sum(kb, axis=0, keepdims=True) * (1.0 / blk)
            return c

        lax.fori_loop(0, n_blocks, fill, 0)
        put_query(q_ref, 0, 0)

    cur = t % 2
    q_aug = [qaug_ref[cur, h] for h in heads]
    key_i = lax.broadcasted_iota(jnp.int32, (blk, blk), 0)
    qry_i = lax.broadcasted_iota(jnp.int32, (blk, blk), 1)
    causal = key_i <= qry_i
    own = pl.ds(pl.multiple_of(t * blk, blk), blk)
    k_own = kaug_ref[own, 0:LANES]
    last_chunk = n_blocks // chunk - 1

    def put_scores(c, slot):
        rows = pl.ds(pl.multiple_of(c * (chunk * blk), chunk * blk), chunk * blk)
        kc = kaug_ref[rows, :]
        for h in heads:
            s_ref[slot, h] = _dot(kc, q_aug[h])

    def consume(c, slot, state):
        new = []
        for h in heads:
            m, acc = state[h]
            s = s_ref[slot, h]
            m_new = jnp.maximum(m, jnp.max(s, axis=0, keepdims=True))
            p = jnp.exp2(s - m_new).astype(BF16)
            acc = jnp.exp2(m - m_new) * acc
            for i in range(chunk):
                acc = acc + _dot(vt_ref[c * chunk + i, h], p[i * blk:(i + 1) * blk, :])
            new.append((m_new, acc))
        return tuple(new)

    own_scores = [_dot(k_own, q_aug[h][0:LANES]) for h in heads]
    put_scores(0, 0)
    state = []
    for h in heads:
        s = jnp.where(causal, own_scores[h], MASK_VALUE)
        m = jnp.max(s, axis=0, keepdims=True)
        state.append((m, _dot(vt_ref[t, h], jnp.exp2(s - m).astype(BF16))))
    put_query(qn_ref, t + 1, 1 - cur)

    n_chunks = (t + chunk - 1) // chunk

    def step_pair(j, state):
        put_scores(2 * j + 1, 1)
        state = consume(2 * j, 0, state)
        put_scores(jnp.minimum(2 * j + 2, last_chunk), 0)
        return consume(2 * j + 1, 1, state)

    state = lax.fori_loop(0, n_chunks // 2, step_pair, tuple(state))
    fin = lax.cond(n_chunks % 2 == 1, lambda st: consume(n_chunks - 1, 0, st), lambda st: st, state)
    o_t = jnp.concatenate([acc[0:HEAD_DIM] * (1.0 / acc[HEAD_DIM:HEAD_DIM + 1]) for (_, acc) in fin],
                          axis=0)
    o_ref[...] = o_t.T.astype(o_ref.dtype)


def _attn_prompt(q, k, v):
    bsz, s, w = q.shape
    n_blocks = s // MOBA_BLOCK
    chunk = 4
    assert n_blocks % (2 * chunk) == 0 and n_blocks % 16 == 0 and n_blocks <= LANES
    groups = w // LANES
    q_blk = pl.BlockSpec((None, MOBA_BLOCK, LANES), lambda b, g, t: (b, t, g))
    q_next = pl.BlockSpec((None, MOBA_BLOCK, LANES),
                          lambda b, g, t: (b, jnp.minimum(t + 1, n_blocks - 1), g))
    kv_blk = pl.BlockSpec((None, s, LANES), lambda b, g, t: (b, 0, g))
    return pl.pallas_call(
        functools.partial(_attn_prompt_kernel, n_blocks=n_blocks, chunk=chunk),
        grid=(bsz, groups, n_blocks),
        in_specs=[q_blk, q_next, kv_blk, kv_blk],
        out_specs=q_blk,
        out_shape=jax.ShapeDtypeStruct((bsz, s, w), BF16),
        scratch_shapes=[
            pltpu.VMEM((s, 2 * LANES), BF16),
            pltpu.VMEM((n_blocks, HEADS_PER_GROUP, V_ROWS, MOBA_BLOCK), BF16),
            pltpu.VMEM((n_blocks, LANES), F32),
            pltpu.VMEM((2, HEADS_PER_GROUP, 2 * LANES, MOBA_BLOCK), BF16),
            pltpu.VMEM((2, HEADS_PER_GROUP, chunk * MOBA_BLOCK, MOBA_BLOCK), F32),
        ],
        compiler_params=pltpu.CompilerParams(
            dimension_semantics=("arbitrary", "arbitrary", "arbitrary"),
            vmem_limit_bytes=VMEM_LIMIT),
        name="attn_prompt",
    )(q, q, k, v)


def _attn_sample_kernel(pt_ref, q_ref, kn_ref, vn_ref, *rest, n_pages, n_new):
    del pt_ref
    k_refs = rest[:n_pages]
    v_refs = rest[n_pages:2 * n_pages]
    o_ref = rest[2 * n_pages]
    pages_per_blk = MOBA_BLOCK // PAGE_SIZE
    n_blk = n_pages // pages_per_blk

    q = q_ref[...]
    r = lax.broadcasted_iota(jnp.int32, q.shape, 0)
    lane = lax.broadcasted_iota(jnp.int32, q.shape, 1)
    diag = (lane // HEAD_DIM) == (r // n_new)
    qbd = jnp.where(diag, q, jnp.zeros_like(q))
    qf = qbd.astype(F32)

    s_page = [_dot(qbd, k_refs[p][...].astype(BF16)) for p in range(n_pages)]
    sb = []
    for j in range(n_blk):
        tot = s_page[j * pages_per_blk]
        for i in range(1, pages_per_blk):
            tot = tot + s_page[j * pages_per_blk + i]
        sb.append(jnp.sum(tot, axis=-1, keepdims=True) * (1.0 / MOBA_BLOCK))

    sel = []
    for j in range(n_blk):
        rank = jnp.zeros(sb[j].shape, jnp.int32)
        for i in range(n_blk):
            if i == j:
                continue
            beats = (sb[i] >= sb[j]) if i < j else (sb[i] > sb[j])
            rank = rank + beats.astype(jnp.int32)
        sel.append(rank < MOBA_TOPK)

    tok = r[:, 0:1] % n_new
    s_past = [jnp.where(sel[p // pages_per_blk], s_page[p], MASK_VALUE) for p in range(n_pages)]
    kn = kn_ref[...]
    vn = vn_ref[...]
    s_new = [jnp.where(tok >= i, jnp.sum(qf * kn[i:i + 1, :], axis=-1, keepdims=True), MASK_VALUE)
             for i in range(n_new)]
    m = s_new[0]
    for s in s_new[1:]:
        m = jnp.maximum(m, s)
    for s in s_past:
        m = jnp.maximum(m, jnp.max(s, axis=-1, keepdims=True))

    l = jnp.zeros(m.shape, F32)
    acc = jnp.zeros(q.shape, F32)
    for i in range(n_new):
        p = jnp.exp2(s_new[i] - m)
        l = l + p
        acc = acc + p * vn[i:i + 1, :]
    for pg in range(n_pages):
        p = jnp.exp2(s_past[pg] - m)
        l = l + jnp.sum(p, axis=-1, keepdims=True)
        acc = acc + _dot_nt(p.astype(BF16), v_refs[pg][...].astype(BF16))
    o = jnp.where(diag, acc / l, 0.0)
    while o.shape[0] > 8:
        half = o.shape[0] // 2
        o = o[:half] + o[half:]
    shift = 4
    while shift >= n_new:
        o = o + pltpu.roll(o, shift, axis=0)
        shift //= 2
    o_ref[...] = o.astype(o_ref.dtype)


def _attn_sample(q, k_new, v_new, cache_k, cache_v, page_table):
    n, t, w = q.shape
    n_pages = page_table.shape[1]
    assert (n_pages * PAGE_SIZE) % MOBA_BLOCK == 0 and MOBA_BLOCK % PAGE_SIZE == 0
    assert t in (1, 2, 4, 8) and n_pages * PAGE_SIZE // MOBA_BLOCK >= MOBA_TOPK
    n_pool = cache_k.shape[0]
    ck = cache_k.transpose(0, 2, 3, 1).reshape(n_pool, w, PAGE_SIZE)
    cv = cache_v.transpose(0, 2, 3, 1).reshape(n_pool, w, PAGE_SIZE)
    q_rep = jnp.tile(q, (1, N_HEADS, 1))
    pad = ((0, 0), (0, 8 - t), (0, 0))
    kn = jnp.pad(k_new, pad)
    vn = jnp.pad(v_new, pad)

    def page_spec(p):
        return pl.BlockSpec((None, w, PAGE_SIZE), lambda i, pt: (pt[i * n_pages + p], 0, 0))

    grid_spec = pltpu.PrefetchScalarGridSpec(
        num_scalar_prefetch=1,
        grid=(n,),
        in_specs=[
            pl.BlockSpec((None, N_HEADS * t, w), lambda i, pt: (i, 0, 0)),
            pl.BlockSpec((None, 8, w), lambda i, pt: (i, 0, 0)),
            pl.BlockSpec((None, 8, w), lambda i, pt: (i, 0, 0)),
        ] + [page_spec(p) for p in range(n_pages)] * 2,
        out_specs=pl.BlockSpec((None, 8, w), lambda i, pt: (i, 0, 0)),
    )
    out = pl.pallas_call(
        functools.partial(_attn_sample_kernel, n_pages=n_pages, n_new=t),
        grid_spec=grid_spec,
        out_shape=jax.ShapeDtypeStruct((n, 8, w), BF16),
        compiler_params=pltpu.CompilerParams(
            dimension_semantics=("arbitrary",), vmem_limit_bytes=VMEM_LIMIT),
        name="attn_sample",
    )(page_table.reshape(-1), q_rep, kn, vn, *([ck] * n_pages), *([cv] * n_pages))
    return out[:, :t]


def _merge_ffn_kernel(x_ref, ao_ref, co_ref, ga_ref, gc_ref, wpa_ref, wpc_ref, wo_ref,
                      g1_ref, g2_ref, g3_ref, wg_ref, wu_ref, wd_ref, y_ref, a_ref, *, ff_chunk):
    m = ga_ref[...] * _dot(ao_ref[...], wpa_ref[...]) + gc_ref[...] * _dot(co_ref[...], wpc_ref[...])
    h = x_ref[...] + _rms(_dot(m.astype(BF16), wo_ref[...])) * g1_ref[...]
    hn = (_rms(h) * g2_ref[...]).astype(BF16)
    d_ff = wg_ref.shape[1]
    for c in range(0, d_ff, ff_chunk):
        g = _dot(hn, wg_ref[:, c:c + ff_chunk])
        u = _dot(hn, wu_ref[:, c:c + ff_chunk])
        a_ref[:, c:c + ff_chunk] = (g * jax.nn.sigmoid(g) * u).astype(BF16)
    f = _dot(a_ref[...], wd_ref[...])
    y_ref[...] = h + _rms(f) * g3_ref[...]


def _merge_ffn(x, ao, co, ga, gc, wpa, wpc, wo, g1, g2, g3, wg, wu, wd, tm):
    rows, d = x.shape
    cw = ao.shape[1]
    d_ff = wg.shape[1]
    ff_chunk = 256
    assert d_ff % ff_chunk == 0 and rows % tm == 0
    row_blk = lambda width: pl.BlockSpec((tm, width), lambda i: (i, 0))
    return pl.pallas_call(
        functools.partial(_merge_ffn_kernel, ff_chunk=ff_chunk),
        grid=(rows // tm,),
        in_specs=[row_blk(d), row_blk(cw), row_blk(cw), row_blk(d), row_blk(d),
                  _resident(wpa.shape), _resident(wpc.shape), _resident(wo.shape),
                  _resident((1, d)), _resident((1, d)), _resident((1, d)),
                  _resident(wg.shape), _resident(wu.shape), _resident(wd.shape)],
        out_specs=row_blk(d),
        out_shape=jax.ShapeDtypeStruct((rows, d), F32),
        scratch_shapes=[pltpu.VMEM((tm, d_ff), BF16)],
        compiler_params=pltpu.CompilerParams(
            dimension_semantics=("arbitrary",), vmem_limit_bytes=VMEM_LIMIT),
        name="merge_ffn",
    )(x, ao, co, ga, gc, wpa, wpc, wo, g1.reshape(1, d), g2.reshape(1, d), g3.reshape(1, d),
      wg, wu, wd)


def kernel(x_prompt, x_sample, cache_k, cache_v, state_conv, page_table, g_attn_pre, w_in, w_conv, w_proj_attn, w_proj_conv, w_out, g_attn_post, g_ffn_pre, w_gate, w_up, w_down, g_ffn_post):
    depth = w_in.shape[0]
    bsz, s, d = x_prompt.shape
    n, t, _ = x_sample.shape
    cw = ATTN_WIDTH
    assert t >= CONV_WIDTH - 1 and s >= CONV_WIDTH - 1
    xp, xs = x_prompt, x_sample
    outs = [[] for _ in range(6)]
    for l in range(depth):
        w_in_b = w_in[l].astype(BF16)
        tail = (w_proj_attn[l].astype(BF16), w_proj_conv[l].astype(BF16), w_out[l].astype(BF16),
                g_attn_post[l], g_ffn_pre[l], g_ffn_post[l],
                w_gate[l].astype(BF16), w_up[l].astype(BF16), w_down[l].astype(BF16))

        prev0 = jnp.zeros((bsz, CONV_WIDTH - 1, cw), F32)
        q, k, v, co, ga, gc, u_tail = _proj_prompt(xp, g_attn_pre[l], w_in_b, w_conv[l], prev0, 512)
        ao = _attn_prompt(q, k, v)
        flat = lambda a: a.reshape(bsz * s, a.shape[-1])
        xp = _merge_ffn(flat(xp), flat(ao), flat(co), flat(ga), flat(gc), *tail, 512)
        xp = xp.reshape(bsz, s, d)
        outs[0].append(k.reshape(bsz, s, N_HEADS, HEAD_DIM))
        outs[1].append(v.reshape(bsz, s, N_HEADS, HEAD_DIM))
        outs[2].append(u_tail[:, 8 - (CONV_WIDTH - 1):])

        q, k, v, co, ga, gc, u = _proj_sample(xs, g_attn_pre[l], w_in_b, w_conv[l], state_conv[l])
        ao = _attn_sample(q.reshape(n, t, cw), k.reshape(n, t, cw), v.reshape(n, t, cw),
                          cache_k[l], cache_v[l], page_table)
        xs = _merge_ffn(xs.reshape(n * t, d), ao.reshape(n * t, cw), co, ga, gc, *tail, 256)
        xs = xs.reshape(n, t, d)
        outs[3].append(k.reshape(n, t, N_HEADS, HEAD_DIM))
        outs[4].append(v.reshape(n, t, N_HEADS, HEAD_DIM))
        outs[5].append(u.reshape(n, t, cw)[:, t - (CONV_WIDTH - 1):])

    kp, vp, cp, ks, vs, cs = (jnp.stack(o) for o in outs)
    return (xp, xs, kp, vp, cp, ks, vs, cs)
```

```python
import functools

import jax
import jax.numpy as jnp
from jax import lax
from jax.experimental import pallas as pl
from jax.experimental.pallas import tpu as pltpu

N_HEADS = 8
HEAD_DIM = 64
ATTN_WIDTH = N_HEADS * HEAD_DIM
MOBA_BLOCK = 256
MOBA_TOPK = 3
CONV_WIDTH = 3
PAGE_SIZE = 128
RMS_EPS = 1e-6

LANES = 128
HEADS_PER_GROUP = LANES // HEAD_DIM
MASK_VALUE = -(2.0 ** 100)
Q_SCALE = HEAD_DIM ** -0.5 * 1.4426950408889634
VMEM_LIMIT = 52 * 1024 * 1024
PAGE_SLOTS = 3

F32 = jnp.float32
BF16 = jnp.bfloat16


def _dot(a, b):
    return jnp.dot(a, b, preferred_element_type=F32)


def _dot_nt(a, b):
    return lax.dot_general(a, b, (((1,), (1,)), ((), ())), preferred_element_type=F32)


def _rms(x):
    return x * lax.rsqrt(jnp.mean(x * x, axis=-1, keepdims=True) + RMS_EPS)


def _resident(shape):
    return pl.BlockSpec(shape, lambda *_: (0,) * len(shape), pipeline_mode=pl.Buffered(1))


def _proj_kernel(x_ref, g_ref, w_ref, wc_ref, b1_ref, b2_ref,
                 q_ref, k_ref, v_ref, co_ref, ga_ref, gc_ref, u_ref,
                 carry_ref, *, seq_rows, d_model):
    tm = x_ref.shape[0]
    cw = ATTN_WIDTH
    xn = (_rms(x_ref[...]) * g_ref[...]).astype(BF16)

    def proj(lo, width):
        return _dot(xn, w_ref[:, lo:lo + width])

    q_ref[...] = (proj(0, cw) * Q_SCALE).astype(BF16)
    k_ref[...] = proj(cw, cw)
    v_ref[...] = proj(2 * cw, cw)
    cc = proj(4 * cw, cw)
    cx = proj(5 * cw, cw)
    u = cc * cx
    row = lax.broadcasted_iota(jnp.int32, u.shape, 0)
    s1 = pltpu.roll(u, 1, axis=0)
    s2 = pltpu.roll(u, 2, axis=0)
    if seq_rows is None:
        @pl.when(pl.program_id(1) == 0)
        def _():
            carry_ref[...] = b1_ref[0]
        c = carry_ref[...]
        prev1 = jnp.broadcast_to(c[7:8, :], u.shape)
        prev2 = jnp.broadcast_to(c[6:7, :], u.shape)
        s1 = jnp.where(row == 0, prev1, s1)
        s2 = jnp.where(row == 0, prev2, jnp.where(row == 1, prev1, s2))
        carry_ref[...] = u[tm - 8:tm, :]
        u_ref[0] = u[tm - 8:tm, :]
    else:
        t = row % seq_rows
        s1 = jnp.where(t >= 1, s1, b1_ref[...])
        s2 = jnp.where(t >= 2, s2, b2_ref[...])
        u_ref[...] = u
    wc = wc_ref[...]
    y = wc[0:1, :] * s2 + wc[1:2, :] * s1 + wc[2:3, :] * u
    co_ref[...] = (proj(3 * cw, cw) * y).astype(BF16)
    ga_ref[...] = jax.nn.sigmoid(proj(6 * cw, d_model))
    gc_ref[...] = jax.nn.sigmoid(proj(6 * cw + d_model, d_model))


def _proj_prompt(x, g, w_in, w_conv, prev, tm):
    bsz, s, d = x.shape
    cw = ATTN_WIDTH
    hist = jnp.zeros((bsz, 8, cw), F32).at[:, 8 - (CONV_WIDTH - 1):].set(prev)
    row_blk = lambda width: pl.BlockSpec((None, tm, width), lambda b, i: (b, i, 0))
    outs = pl.pallas_call(
        functools.partial(_proj_kernel, seq_rows=None, d_model=d),
        grid=(bsz, s // tm),
        in_specs=[
            row_blk(d),
            _resident((1, d)),
            _resident(w_in.shape),
            _resident(w_conv.shape),
            pl.BlockSpec((1, 8, cw), lambda b, i: (b, 0, 0)),
            pl.BlockSpec((1, 8, cw), lambda b, i: (b, 0, 0)),
        ],
        out_specs=[row_blk(cw), row_blk(cw), row_blk(cw), row_blk(cw), row_blk(d), row_blk(d),
                   pl.BlockSpec((1, 8, cw), lambda b, i: (b, 0, 0))],
        out_shape=[
            jax.ShapeDtypeStruct((bsz, s, cw), BF16),
            jax.ShapeDtypeStruct((bsz, s, cw), F32),
            jax.ShapeDtypeStruct((bsz, s, cw), F32),
            jax.ShapeDtypeStruct((bsz, s, cw), BF16),
            jax.ShapeDtypeStruct((bsz, s, d), F32),
            jax.ShapeDtypeStruct((bsz, s, d), F32),
            jax.ShapeDtypeStruct((bsz, 8, cw), F32),
        ],
        scratch_shapes=[pltpu.VMEM((8, cw), F32)],
        compiler_params=pltpu.CompilerParams(
            dimension_semantics=("arbitrary", "arbitrary"), vmem_limit_bytes=VMEM_LIMIT),
        name="proj_prompt",
    )(x, g.reshape(1, d), w_in, w_conv, hist, hist)
    return outs


def _proj_sample(x, g, w_in, w_conv, state):
    n, t, d = x.shape
    cw = ATTN_WIDTH
    rows = n * t
    b1 = jnp.zeros((n, t, cw), F32).at[:, 0].set(state[:, 1]).reshape(rows, cw)
    b2 = jnp.zeros((n, t, cw), F32).at[:, 0].set(state[:, 0]).at[:, 1].set(state[:, 1])
    b2 = b2.reshape(rows, cw)
    full = lambda width: pl.BlockSpec((rows, width), lambda i: (0, 0))
    outs = pl.pallas_call(
        functools.partial(_proj_kernel, seq_rows=t, d_model=d),
        grid=(1,),
        in_specs=[full(d), _resident((1, d)), _resident(w_in.shape), _resident(w_conv.shape),
                  full(cw), full(cw)],
        out_specs=[full(cw), full(cw), full(cw), full(cw), full(d), full(d), full(cw)],
        out_shape=[
            jax.ShapeDtypeStruct((rows, cw), BF16),
            jax.ShapeDtypeStruct((rows, cw), F32),
            jax.ShapeDtypeStruct((rows, cw), F32),
            jax.ShapeDtypeStruct((rows, cw), BF16),
            jax.ShapeDtypeStruct((rows, d), F32),
            jax.ShapeDtypeStruct((rows, d), F32),
            jax.ShapeDtypeStruct((rows, cw), F32),
        ],
        scratch_shapes=[pltpu.VMEM((8, cw), F32)],
        compiler_params=pltpu.CompilerParams(
            dimension_semantics=("arbitrary",), vmem_limit_bytes=VMEM_LIMIT),
        name="proj_sample",
    )(x.reshape(rows, d), g.reshape(1, d), w_in, w_conv, b1, b2)
    return outs


V_ROWS = HEAD_DIM + 16


def _select_blocks(scores, n_valid):
    blk_id = lax.broadcasted_iota(jnp.int32, scores.shape, 0)
    blk_f = blk_id.astype(F32)
    cand = jnp.where(blk_id < n_valid, scores, -jnp.inf)
    sel = jnp.zeros(scores.shape, jnp.bool_)
    for _ in range(MOBA_TOPK):
        mx = jnp.max(cand, axis=0, keepdims=True)
        idx = jnp.min(jnp.where(cand == mx, blk_f, float(scores.shape[0])), axis=0, keepdims=True)
        pick = (blk_f == idx) & (mx > -jnp.inf)
        sel = sel | pick
        cand = jnp.where(pick, -jnp.inf, cand)
    return sel


def _attn_prompt_kernel(q_ref, qn_ref, k_ref, v_ref, o_ref, kaug_ref, vt_ref, kmean_ref, qaug_ref,
                        s_ref, *, n_blocks, chunk):
    t = pl.program_id(2)
    blk = MOBA_BLOCK
    heads = range(HEADS_PER_GROUP)

    def put_query(q_rows_ref, n_valid, slot):
        q_t = q_rows_ref[...].astype(F32).T
        d_row = lax.broadcasted_iota(jnp.int32, q_t.shape, 0)
        kmean = kmean_ref[...].astype(BF16)
        pad_rows = jnp.zeros((LANES - n_blocks, blk), BF16)
        for h in heads:
            in_head = (d_row >= h * HEAD_DIM) & (d_row < (h + 1) * HEAD_DIM)
            qh = jnp.where(in_head, q_t, 0.0).astype(BF16)
            pen = jnp.where(_select_blocks(_dot(kmean, qh), n_valid), 0.0, MASK_VALUE).astype(BF16)
            qaug_ref[slot, h] = jnp.concatenate([qh, pen, pad_rows], axis=0)

    @pl.when(t == 0)
    def _init():
        lane_k = lax.broadcasted_iota(jnp.int32, (blk, LANES), 1)
        pad_i = lax.broadcasted_iota(jnp.int32, (V_ROWS - HEAD_DIM, blk), 0)
        ones_rows = (pad_i == 0).astype(BF16)

        def fill(j, c):
            rows = pl.ds(pl.multiple_of(j * blk, blk), blk)
            kb = k_ref[rows, :]
            kaug_ref[rows, 0:LANES] = kb.astype(BF16)
            kaug_ref[rows, LANES:2 * LANES] = (lane_k == j).astype(BF16)
            v_t = v_ref[rows, :].T
            for h in heads:
                vt_ref[j, h, 0:HEAD_DIM] = v_t[h * HEAD_DIM:(h + 1) * HEAD_DIM, :].astype(BF16)
                vt_ref[j, h, HEAD_DIM:V_ROWS] = ones_rows
            kmean_ref[pl.ds(j, 1), :] = jnp.sum(kb, axis=0, keepdims=True) * (1.0 / blk)
            return c

        lax.fori_loop(0, n_blocks, fill, 0)
        put_query(q_ref, 0, 0)

    cur = t % 2
    q_aug = [qaug_ref[cur, h] for h in heads]
    key_i = lax.broadcasted_iota(jnp.int32, (blk, blk), 0)
    qry_i = lax.broadcasted_iota(jnp.int32, (blk, blk), 1)
    causal = key_i <= qry_i
    own = pl.ds(pl.multiple_of(t * blk, blk), blk)
    k_own = kaug_ref[own, 0:LANES]
    last_chunk = n_blocks // chunk - 1

    def put_scores(c, slot):
        rows = pl.ds(pl.multiple_of(c * (chunk * blk), chunk * blk), chunk * blk)
        kc = kaug_ref[rows, :]
        for h in heads:
            s_ref[slot, h] = _dot(kc, q_aug[h])

    def consume(c, slot, state):
        new = []
        for h in heads:
            m, acc = state[h]
            s = s_ref[slot, h]
            m_new = jnp.maximum(m, jnp.max(s, axis=0, keepdims=True))
            p = jnp.exp2(s - m_new).astype(BF16)
            acc = jnp.exp2(m - m_new) * acc
            for i in range(chunk):
                acc = acc + _dot(vt_ref[c * chunk + i, h], p[i * blk:(i + 1) * blk, :])
            new.append((m_new, acc))
        return tuple(new)

    own_scores = [_dot(k_own, q_aug[h][0:LANES]) for h in heads]
    put_scores(0, 0)
    state = []
    for h in heads:
        s = jnp.where(causal, own_scores[h], MASK_VALUE)
        m = jnp.max(s, axis=0, keepdims=True)
        state.append((m, _dot(vt_ref[t, h], jnp.exp2(s - m).astype(BF16))))
    put_query(qn_ref, t + 1, 1 - cur)

    n_chunks = (t + chunk - 1) // chunk

    def step_pair(j, state):
        put_scores(2 * j + 1, 1)
        state = consume(2 * j, 0, state)
        put_scores(jnp.minimum(2 * j + 2, last_chunk), 0)
        return consume(2 * j + 1, 1, state)

    state = lax.fori_loop(0, n_chunks // 2, step_pair, tuple(state))
    fin = lax.cond(n_chunks % 2 == 1, lambda st: consume(n_chunks - 1, 0, st), lambda st: st, state)
    o_t = jnp.concatenate([acc[0:HEAD_DIM] * (1.0 / acc[HEAD_DIM:HEAD_DIM + 1]) for (_, acc) in fin],
                          axis=0)
    o_ref[...] = o_t.T.astype(o_ref.dtype)


def _attn_prompt(q, k, v):
    bsz, s, w = q.shape
    n_blocks = s // MOBA_BLOCK
    chunk = 4
    assert n_blocks % (2 * chunk) == 0 and n_blocks % 16 == 0 and n_blocks <= LANES
    groups = w // LANES
    q_blk = pl.BlockSpec((None, MOBA_BLOCK, LANES), lambda b, g, t: (b, t, g))
    q_next = pl.BlockSpec((None, MOBA_BLOCK, LANES),
                          lambda b, g, t: (b, jnp.minimum(t + 1, n_blocks - 1), g))
    kv_blk = pl.BlockSpec((None, s, LANES), lambda b, g, t: (b, 0, g))
    return pl.pallas_call(
        functools.partial(_attn_prompt_kernel, n_blocks=n_blocks, chunk=chunk),
        grid=(bsz, groups, n_blocks),
        in_specs=[q_blk, q_next, kv_blk, kv_blk],
        out_specs=q_blk,
        out_shape=jax.ShapeDtypeStruct((bsz, s, w), BF16),
        scratch_shapes=[
            pltpu.VMEM((s, 2 * LANES), BF16),
            pltpu.VMEM((n_blocks, HEADS_PER_GROUP, V_ROWS, MOBA_BLOCK), BF16),
            pltpu.VMEM((n_blocks, LANES), F32),
            pltpu.VMEM((2, HEADS_PER_GROUP, 2 * LANES, MOBA_BLOCK), BF16),
            pltpu.VMEM((2, HEADS_PER_GROUP, chunk * MOBA_BLOCK, MOBA_BLOCK), F32),
        ],
        compiler_params=pltpu.CompilerParams(
            dimension_semantics=("arbitrary", "arbitrary", "arbitrary"),
            vmem_limit_bytes=VMEM_LIMIT),
        name="attn_prompt",
    )(q, q, k, v)


def _attn_sample_kernel(pt_ref, q_ref, kn_ref, vn_ref, ck_hbm, cv_hbm, o_ref, kbuf, vbuf, sem, *,
                        n_pages, n_new, n_seq):
    i = pl.program_id(0)
    pages_per_blk = MOBA_BLOCK // PAGE_SIZE
    n_blk = n_pages // pages_per_blk

    def page_copies(seq):
        slot = seq % PAGE_SLOTS
        copies = []
        for p in range(n_pages):
            page = pt_ref[seq * n_pages + p]
            copies.append(pltpu.make_async_copy(ck_hbm.at[page], kbuf.at[slot, p], sem.at[0, slot, p]))
            copies.append(pltpu.make_async_copy(cv_hbm.at[page], vbuf.at[slot, p], sem.at[1, slot, p]))
        return copies

    @pl.when(i == 0)
    def _():
        for ahead in range(min(PAGE_SLOTS - 1, n_seq)):
            for c in page_copies(ahead):
                c.start()

    @pl.when(i + PAGE_SLOTS - 1 < n_seq)
    def _():
        for c in page_copies(i + PAGE_SLOTS - 1):
            c.start()

    for c in page_copies(i):
        c.wait()
    slot = i % PAGE_SLOTS
    k_refs = [kbuf.at[slot, p] for p in range(n_pages)]
    v_refs = [vbuf.at[slot, p] for p in range(n_pages)]

    q = q_ref[...]
    r = lax.broadcasted_iota(jnp.int32, q.shape, 0)
    lane = lax.broadcasted_iota(jnp.int32, q.shape, 1)
    diag = (lane // HEAD_DIM) == (r // n_new)
    qbd = jnp.where(diag, q, jnp.zeros_like(q))
    qf = qbd.astype(F32)

    def block_of(refs, j):
        pages = [refs[j * pages_per_blk + i][...].astype(BF16) for i in range(pages_per_blk)]
        return jnp.concatenate(pages, axis=1)

    s_blk = [_dot(qbd, block_of(k_refs, j)) for j in range(n_blk)]
    sb = [jnp.sum(s, axis=-1, keepdims=True) * (1.0 / MOBA_BLOCK) for s in s_blk]

    sel = []
    for j in range(n_blk):
        rank = jnp.zeros(sb[j].shape, jnp.int32)
        for i in range(n_blk):
            if i == j:
                continue
            beats = (sb[i] >= sb[j]) if i < j else (sb[i] > sb[j])
            rank = rank + beats.astype(jnp.int32)
        sel.append(rank < MOBA_TOPK)

    tok = r[:, 0:1] % n_new
    s_past = [jnp.where(sel[j], s_blk[j], MASK_VALUE) for j in range(n_blk)]
    kn = kn_ref[...]
    vn = vn_ref[...]
    s_new = [jnp.where(tok >= i, jnp.sum(qf * kn[i:i + 1, :], axis=-1, keepdims=True), MASK_VALUE)
             for i in range(n_new)]
    m = s_new[0]
    for s in s_new[1:]:
        m = jnp.maximum(m, s)
    for s in s_past:
        m = jnp.maximum(m, jnp.max(s, axis=-1, keepdims=True))

    l = jnp.zeros(m.shape, F32)
    acc = jnp.zeros(q.shape, F32)
    for i in range(n_new):
        p = jnp.exp2(s_new[i] - m)
        l = l + p
        acc = acc + p * vn[i:i + 1, :]
    for j in range(n_blk):
        p = jnp.exp2(s_past[j] - m)
        l = l + jnp.sum(p, axis=-1, keepdims=True)
        acc = acc + _dot_nt(p.astype(BF16), block_of(v_refs, j))
    o = jnp.where(diag, acc / l, 0.0)
    while o.shape[0] > 8:
        half = o.shape[0] // 2
        o = o[:half] + o[half:]
    shift = 4
    while shift >= n_new:
        o = o + pltpu.roll(o, shift, axis=0)
        shift //= 2
    o_ref[...] = o.astype(o_ref.dtype)


def _attn_sample(q, k_new, v_new, cache_k, cache_v, page_table):
    n, t, w = q.shape
    n_pages = page_table.shape[1]
    assert (n_pages * PAGE_SIZE) % MOBA_BLOCK == 0 and MOBA_BLOCK % PAGE_SIZE == 0
    assert t in (1, 2, 4, 8) and n_pages * PAGE_SIZE // MOBA_BLOCK >= MOBA_TOPK
    n_pool = cache_k.shape[0]
    ck = cache_k.transpose(0, 2, 3, 1).reshape(n_pool, w, PAGE_SIZE)
    cv = cache_v.transpose(0, 2, 3, 1).reshape(n_pool, w, PAGE_SIZE)
    q_rep = jnp.tile(q, (1, N_HEADS, 1))
    pad = ((0, 0), (0, 8 - t), (0, 0))
    kn = jnp.pad(k_new, pad)
    vn = jnp.pad(v_new, pad)

    grid_spec = pltpu.PrefetchScalarGridSpec(
        num_scalar_prefetch=1,
        grid=(n,),
        in_specs=[
            pl.BlockSpec((None, N_HEADS * t, w), lambda i, pt: (i, 0, 0)),
            pl.BlockSpec((None, 8, w), lambda i, pt: (i, 0, 0)),
            pl.BlockSpec((None, 8, w), lambda i, pt: (i, 0, 0)),
            pl.BlockSpec(memory_space=pl.ANY),
            pl.BlockSpec(memory_space=pl.ANY),
        ],
        out_specs=pl.BlockSpec((None, 8, w), lambda i, pt: (i, 0, 0)),
        scratch_shapes=[
            pltpu.VMEM((PAGE_SLOTS, n_pages, w, PAGE_SIZE), F32),
            pltpu.VMEM((PAGE_SLOTS, n_pages, w, PAGE_SIZE), F32),
            pltpu.SemaphoreType.DMA((2, PAGE_SLOTS, n_pages)),
        ],
    )
    out = pl.pallas_call(
        functools.partial(_attn_sample_kernel, n_pages=n_pages, n_new=t, n_seq=n),
        grid_spec=grid_spec,
        out_shape=jax.ShapeDtypeStruct((n, 8, w), BF16),
        compiler_params=pltpu.CompilerParams(
            dimension_semantics=("arbitrary",), vmem_limit_bytes=VMEM_LIMIT),
        name="attn_sample",
    )(page_table.reshape(-1), q_rep, kn, vn, ck, cv)
    return out[:, :t]


def _merge_ffn_kernel(x_ref, ao_ref, co_ref, ga_ref, gc_ref, wpa_ref, wpc_ref, wo_ref,
                      g1_ref, g2_ref, g3_ref, wg_ref, wu_ref, wd_ref, y_ref, a_ref, *, ff_chunk):
    m = ga_ref[...] * _dot(ao_ref[...], wpa_ref[...]) + gc_ref[...] * _dot(co_ref[...], wpc_ref[...])
    h = x_ref[...] + _rms(_dot(m.astype(BF16), wo_ref[...])) * g1_ref[...]
    hn = (_rms(h) * g2_ref[...]).astype(BF16)
    d_ff = wg_ref.shape[1]
    for c in range(0, d_ff, ff_chunk):
        g = _dot(hn, wg_ref[:, c:c + ff_chunk])
        u = _dot(hn, wu_ref[:, c:c + ff_chunk])
        a_ref[:, c:c + ff_chunk] = (g * jax.nn.sigmoid(g) * u).astype(BF16)
    f = _dot(a_ref[...], wd_ref[...])
    y_ref[...] = h + _rms(f) * g3_ref[...]


def _merge_ffn(x, ao, co, ga, gc, wpa, wpc, wo, g1, g2, g3, wg, wu, wd, tm):
    rows, d = x.shape
    cw = ao.shape[1]
    d_ff = wg.shape[1]
    ff_chunk = 256
    assert d_ff % ff_chunk == 0 and rows % tm == 0
    row_blk = lambda width: pl.BlockSpec((tm, width), lambda i: (i, 0))
    return pl.pallas_call(
        functools.partial(_merge_ffn_kernel, ff_chunk=ff_chunk),
        grid=(rows // tm,),
        in_specs=[row_blk(d), row_blk(cw), row_blk(cw), row_blk(d), row_blk(d),
                  _resident(wpa.shape), _resident(wpc.shape), _resident(wo.shape),
                  _resident((1, d)), _resident((1, d)), _resident((1, d)),
                  _resident(wg.shape), _resident(wu.shape), _resident(wd.shape)],
        out_specs=row_blk(d),
        out_shape=jax.ShapeDtypeStruct((rows, d), F32),
        scratch_shapes=[pltpu.VMEM((tm, d_ff), BF16)],
        compiler_params=pltpu.CompilerParams(
            dimension_semantics=("arbitrary",), vmem_limit_bytes=VMEM_LIMIT),
        name="merge_ffn",
    )(x, ao, co, ga, gc, wpa, wpc, wo, g1.reshape(1, d), g2.reshape(1, d), g3.reshape(1, d),
      wg, wu, wd)


def kernel(x_prompt, x_sample, cache_k, cache_v, state_conv, page_table, g_attn_pre, w_in, w_conv, w_proj_attn, w_proj_conv, w_out, g_attn_post, g_ffn_pre, w_gate, w_up, w_down, g_ffn_post):
    depth = w_in.shape[0]
    bsz, s, d = x_prompt.shape
    n, t, _ = x_sample.shape
    cw = ATTN_WIDTH
    assert t >= CONV_WIDTH - 1 and s >= CONV_WIDTH - 1
    xp, xs = x_prompt, x_sample
    outs = [[] for _ in range(6)]
    for l in range(depth):
        w_in_b = w_in[l].astype(BF16)
        tail = (w_proj_attn[l].astype(BF16), w_proj_conv[l].astype(BF16), w_out[l].astype(BF16),
                g_attn_post[l], g_ffn_pre[l], g_ffn_post[l],
                w_gate[l].astype(BF16), w_up[l].astype(BF16), w_down[l].astype(BF16))

        prev0 = jnp.zeros((bsz, CONV_WIDTH - 1, cw), F32)
        q, k, v, co, ga, gc, u_tail = _proj_prompt(xp, g_attn_pre[l], w_in_b, w_conv[l], prev0, 512)
        ao = _attn_prompt(q, k, v)
        flat = lambda a: a.reshape(bsz * s, a.shape[-1])
        xp = _merge_ffn(flat(xp), flat(ao), flat(co), flat(ga), flat(gc), *tail, 512)
        xp = xp.reshape(bsz, s, d)
        outs[0].append(k.reshape(bsz, s, N_HEADS, HEAD_DIM))
        outs[1].append(v.reshape(bsz, s, N_HEADS, HEAD_DIM))
        outs[2].append(u_tail[:, 8 - (CONV_WIDTH - 1):])

        q, k, v, co, ga, gc, u = _proj_sample(xs, g_attn_pre[l], w_in_b, w_conv[l], state_conv[l])
        ao = _attn_sample(q.reshape(n, t, cw), k.reshape(n, t, cw), v.reshape(n, t, cw),
                          cache_k[l], cache_v[l], page_table)
        xs = _merge_ffn(xs.reshape(n * t, d), ao.reshape(n * t, cw), co, ga, gc, *tail, 256)
        xs = xs.reshape(n, t, d)
        outs[3].append(k.reshape(n, t, N_HEADS, HEAD_DIM))
        outs[4].append(v.reshape(n, t, N_HEADS, HEAD_DIM))
        outs[5].append(u.reshape(n, t, cw)[:, t - (CONV_WIDTH - 1):])

    kp, vp, cp, ks, vs, cs = (jnp.stack(o) for o in outs)
    return (xp, xs, kp, vp, cp, ks, vs, cs)
```

```python
import functools

import jax
import jax.numpy as jnp
from jax import lax
from jax.experimental import pallas as pl
from jax.experimental.pallas import tpu as pltpu

N_HEADS = 8
HEAD_DIM = 64
ATTN_WIDTH = N_HEADS * HEAD_DIM
MOBA_BLOCK = 256
MOBA_TOPK = 3
CONV_WIDTH = 3
PAGE_SIZE = 128
RMS_EPS = 1e-6

LANES = 128
HEADS_PER_GROUP = LANES // HEAD_DIM
MASK_VALUE = -(2.0 ** 100)
Q_SCALE = HEAD_DIM ** -0.5 * 1.4426950408889634
VMEM_LIMIT = 52 * 1024 * 1024
PAGE_SLOTS = 3

F32 = jnp.float32
BF16 = jnp.bfloat16


def _dot(a, b):
    return jnp.dot(a, b, preferred_element_type=F32)


def _dot_nt(a, b):
    return lax.dot_general(a, b, (((1,), (1,)), ((), ())), preferred_element_type=F32)


def _rms(x):
    return x * lax.rsqrt(jnp.mean(x * x, axis=-1, keepdims=True) + RMS_EPS)


def _resident(shape):
    return pl.BlockSpec(shape, lambda *_: (0,) * len(shape), pipeline_mode=pl.Buffered(1))


def _proj_kernel(x_ref, g_ref, w_ref, wc_ref, b1_ref, b2_ref,
                 q_ref, k_ref, v_ref, co_ref, ga_ref, gc_ref, u_ref,
                 carry_ref, *, seq_rows, d_model):
    tm = x_ref.shape[0]
    cw = ATTN_WIDTH
    xn = (_rms(x_ref[...]) * g_ref[...]).astype(BF16)

    def proj(lo, width):
        return _dot(xn, w_ref[:, lo:lo + width])

    q_ref[...] = (proj(0, cw) * Q_SCALE).astype(BF16)
    k_ref[...] = proj(cw, cw)
    v_ref[...] = proj(2 * cw, cw)
    cc = proj(4 * cw, cw)
    cx = proj(5 * cw, cw)
    u = cc * cx
    row = lax.broadcasted_iota(jnp.int32, u.shape, 0)
    s1 = pltpu.roll(u, 1, axis=0)
    s2 = pltpu.roll(u, 2, axis=0)
    if seq_rows is None:
        @pl.when(pl.program_id(1) == 0)
        def _():
            carry_ref[...] = b1_ref[0]
        c = carry_ref[...]
        prev1 = jnp.broadcast_to(c[7:8, :], u.shape)
        prev2 = jnp.broadcast_to(c[6:7, :], u.shape)
        s1 = jnp.where(row == 0, prev1, s1)
        s2 = jnp.where(row == 0, prev2, jnp.where(row == 1, prev1, s2))
        carry_ref[...] = u[tm - 8:tm, :]
        u_ref[0] = u[tm - 8:tm, :]
    else:
        t = row % seq_rows
        s1 = jnp.where(t >= 1, s1, b1_ref[...])
        s2 = jnp.where(t >= 2, s2, b2_ref[...])
        u_ref[...] = u
    wc = wc_ref[...]
    y = wc[0:1, :] * s2 + wc[1:2, :] * s1 + wc[2:3, :] * u
    co_ref[...] = (proj(3 * cw, cw) * y).astype(BF16)
    ga_ref[...] = jax.nn.sigmoid(proj(6 * cw, d_model))
    gc_ref[...] = jax.nn.sigmoid(proj(6 * cw + d_model, d_model))


def _proj_prompt(x, g, w_in, w_conv, prev, tm):
    bsz, s, d = x.shape
    cw = ATTN_WIDTH
    hist = jnp.zeros((bsz, 8, cw), F32).at[:, 8 - (CONV_WIDTH - 1):].set(prev)
    row_blk = lambda width: pl.BlockSpec((None, tm, width), lambda b, i: (b, i, 0))
    outs = pl.pallas_call(
        functools.partial(_proj_kernel, seq_rows=None, d_model=d),
        grid=(bsz, s // tm),
        in_specs=[
            row_blk(d),
            _resident((1, d)),
            _resident(w_in.shape),
            _resident(w_conv.shape),
            pl.BlockSpec((1, 8, cw), lambda b, i: (b, 0, 0)),
            pl.BlockSpec((1, 8, cw), lambda b, i: (b, 0, 0)),
        ],
        out_specs=[row_blk(cw), row_blk(cw), row_blk(cw), row_blk(cw), row_blk(d), row_blk(d),
                   pl.BlockSpec((1, 8, cw), lambda b, i: (b, 0, 0))],
        out_shape=[
            jax.ShapeDtypeStruct((bsz, s, cw), BF16),
            jax.ShapeDtypeStruct((bsz, s, cw), F32),
            jax.ShapeDtypeStruct((bsz, s, cw), F32),
            jax.ShapeDtypeStruct((bsz, s, cw), BF16),
            jax.ShapeDtypeStruct((bsz, s, d), F32),
            jax.ShapeDtypeStruct((bsz, s, d), F32),
            jax.ShapeDtypeStruct((bsz, 8, cw), F32),
        ],
        scratch_shapes=[pltpu.VMEM((8, cw), F32)],
        compiler_params=pltpu.CompilerParams(
            dimension_semantics=("arbitrary", "arbitrary"), vmem_limit_bytes=VMEM_LIMIT),
        name="proj_prompt",
    )(x, g.reshape(1, d), w_in, w_conv, hist, hist)
    return outs


def _proj_sample(x, g, w_in, w_conv, state):
    n, t, d = x.shape
    cw = ATTN_WIDTH
    rows = n * t
    b1 = jnp.zeros((n, t, cw), F32).at[:, 0].set(state[:, 1]).reshape(rows, cw)
    b2 = jnp.zeros((n, t, cw), F32).at[:, 0].set(state[:, 0]).at[:, 1].set(state[:, 1])
    b2 = b2.reshape(rows, cw)
    tm = 128 if rows % 128 == 0 and 128 % t == 0 else rows
    full = lambda width: pl.BlockSpec((tm, width), lambda i: (i, 0))
    outs = pl.pallas_call(
        functools.partial(_proj_kernel, seq_rows=t, d_model=d),
        grid=(rows // tm,),
        in_specs=[full(d), _resident((1, d)), _resident(w_in.shape), _resident(w_conv.shape),
                  full(cw), full(cw)],
        out_specs=[full(cw), full(cw), full(cw), full(cw), full(d), full(d), full(cw)],
        out_shape=[
            jax.ShapeDtypeStruct((rows, cw), BF16),
            jax.ShapeDtypeStruct((rows, cw), F32),
            jax.ShapeDtypeStruct((rows, cw), F32),
            jax.ShapeDtypeStruct((rows, cw), BF16),
            jax.ShapeDtypeStruct((rows, d), F32),
            jax.ShapeDtypeStruct((rows, d), F32),
            jax.ShapeDtypeStruct((rows, cw), F32),
        ],
        scratch_shapes=[pltpu.VMEM((8, cw), F32)],
        compiler_params=pltpu.CompilerParams(
            dimension_semantics=("arbitrary",), vmem_limit_bytes=VMEM_LIMIT),
        name="proj_sample",
    )(x.reshape(rows, d), g.reshape(1, d), w_in, w_conv, b1, b2)
    return outs


V_ROWS = HEAD_DIM + 16


def _select_blocks(scores, n_valid):
    blk_id = lax.broadcasted_iota(jnp.int32, scores.shape, 0)
    blk_f = blk_id.astype(F32)
    cand = jnp.where(blk_id < n_valid, scores, -jnp.inf)
    sel = jnp.zeros(scores.shape, jnp.bool_)
    for _ in range(MOBA_TOPK):
        mx = jnp.max(cand, axis=0, keepdims=True)
        idx = jnp.min(jnp.where(cand == mx, blk_f, float(scores.shape[0])), axis=0, keepdims=True)
        pick = (blk_f == idx) & (mx > -jnp.inf)
        sel = sel | pick
        cand = jnp.where(pick, -jnp.inf, cand)
    return sel


def _attn_prompt_kernel(q_ref, qn_ref, k_ref, v_ref, o_ref, kaug_ref, vt_ref, kmean_ref, qaug_ref,
                        s_ref, *, n_blocks, chunk):
    t = pl.program_id(2)
    blk = MOBA_BLOCK
    heads = range(HEADS_PER_GROUP)

    def put_query(q_rows_ref, n_valid, slot):
        q_t = q_rows_ref[...].astype(F32).T
        d_row = lax.broadcasted_iota(jnp.int32, q_t.shape, 0)
        kmean = kmean_ref[...].astype(BF16)
        pad_rows = jnp.zeros((LANES - n_blocks, blk), BF16)
        for h in heads:
            in_head = (d_row >= h * HEAD_DIM) & (d_row < (h + 1) * HEAD_DIM)
            qh = jnp.where(in_head, q_t, 0.0).astype(BF16)
            pen = jnp.where(_select_blocks(_dot(kmean, qh), n_valid), 0.0, MASK_VALUE).astype(BF16)
            qaug_ref[slot, h] = jnp.concatenate([qh, pen, pad_rows], axis=0)

    @pl.when(t == 0)
    def _init():
        lane_k = lax.broadcasted_iota(jnp.int32, (blk, LANES), 1)
        pad_i = lax.broadcasted_iota(jnp.int32, (V_ROWS - HEAD_DIM, blk), 0)
        ones_rows = (pad_i == 0).astype(BF16)

        def fill(j, c):
            rows = pl.ds(pl.multiple_of(j * blk, blk), blk)
            kb = k_ref[rows, :]
            kaug_ref[rows, 0:LANES] = kb.astype(BF16)
            kaug_ref[rows, LANES:2 * LANES] = (lane_k == j).astype(BF16)
            v_t = v_ref[rows, :].T
            for h in heads:
                vt_ref[j, h, 0:HEAD_DIM] = v_t[h * HEAD_DIM:(h + 1) * HEAD_DIM, :].astype(BF16)
                vt_ref[j, h, HEAD_DIM:V_ROWS] = ones_rows
            kmean_ref[pl.ds(j, 1), :] = jnp.sum(kb, axis=0, keepdims=True) * (1.0 / blk)
            return c

        lax.fori_loop(0, n_blocks, fill, 0)
        put_query(q_ref, 0, 0)

    cur = t % 2
    q_aug = [qaug_ref[cur, h] for h in heads]
    key_i = lax.broadcasted_iota(jnp.int32, (blk, blk), 0)
    qry_i = lax.broadcasted_iota(jnp.int32, (blk, blk), 1)
    causal = key_i <= qry_i
    own = pl.ds(pl.multiple_of(t * blk, blk), blk)
    k_own = kaug_ref[own, 0:LANES]
    last_chunk = n_blocks // chunk - 1

    def put_scores(c, slot):
        rows = pl.ds(pl.multiple_of(c * (chunk * blk), chunk * blk), chunk * blk)
        kc = kaug_ref[rows, :]
        for h in heads:
            s_ref[slot, h] = _dot(kc, q_aug[h])

    def consume(c, slot, state):
        new = []
        for h in heads:
            m, acc = state[h]
            s = s_ref[slot, h]
            m_new = jnp.maximum(m, jnp.max(s, axis=0, keepdims=True))
            p = jnp.exp2(s - m_new).astype(BF16)
            acc = jnp.exp2(m - m_new) * acc
            for i in range(chunk):
                acc = acc + _dot(vt_ref[c * chunk + i, h], p[i * blk:(i + 1) * blk, :])
            new.append((m_new, acc))
        return tuple(new)

    own_scores = [_dot(k_own, q_aug[h][0:LANES]) for h in heads]
    put_scores(0, 0)
    state = []
    for h in heads:
        s = jnp.where(causal, own_scores[h], MASK_VALUE)
        m = jnp.max(s, axis=0, keepdims=True)
        state.append((m, _dot(vt_ref[t, h], jnp.exp2(s - m).astype(BF16))))
    put_query(qn_ref, t + 1, 1 - cur)

    n_chunks = (t + chunk - 1) // chunk

    def step_pair(j, state):
        put_scores(2 * j + 1, 1)
        state = consume(2 * j, 0, state)
        put_scores(jnp.minimum(2 * j + 2, last_chunk), 0)
        return consume(2 * j + 1, 1, state)

    def step_quad(j, state):
        return step_pair(2 * j + 1, step_pair(2 * j, state))

    n_pairs = n_chunks // 2
    state = lax.fori_loop(0, n_pairs // 2, step_quad, tuple(state))
    state = lax.fori_loop(n_pairs // 2 * 2, n_pairs, step_pair, state)
    fin = lax.cond(n_chunks % 2 == 1, lambda st: consume(n_chunks - 1, 0, st), lambda st: st, state)
    o_t = jnp.concatenate([acc[0:HEAD_DIM] * (1.0 / acc[HEAD_DIM:HEAD_DIM + 1]) for (_, acc) in fin],
                          axis=0)
    o_ref[...] = o_t.T.astype(o_ref.dtype)


def _attn_prompt(q, k, v):
    bsz, s, w = q.shape
    n_blocks = s // MOBA_BLOCK
    chunk = 4
    assert n_blocks % (2 * chunk) == 0 and n_blocks % 16 == 0 and n_blocks <= LANES
    groups = w // LANES
    q_blk = pl.BlockSpec((None, MOBA_BLOCK, LANES), lambda b, g, t: (b, t, g))
    q_next = pl.BlockSpec((None, MOBA_BLOCK, LANES),
                          lambda b, g, t: (b, jnp.minimum(t + 1, n_blocks - 1), g))
    kv_blk = pl.BlockSpec((None, s, LANES), lambda b, g, t: (b, 0, g))
    return pl.pallas_call(
        functools.partial(_attn_prompt_kernel, n_blocks=n_blocks, chunk=chunk),
        grid=(bsz, groups, n_blocks),
        in_specs=[q_blk, q_next, kv_blk, kv_blk],
        out_specs=q_blk,
        out_shape=jax.ShapeDtypeStruct((bsz, s, w), BF16),
        scratch_shapes=[
            pltpu.VMEM((s, 2 * LANES), BF16),
            pltpu.VMEM((n_blocks, HEADS_PER_GROUP, V_ROWS, MOBA_BLOCK), BF16),
            pltpu.VMEM((n_blocks, LANES), F32),
            pltpu.VMEM((2, HEADS_PER_GROUP, 2 * LANES, MOBA_BLOCK), BF16),
            pltpu.VMEM((2, HEADS_PER_GROUP, chunk * MOBA_BLOCK, MOBA_BLOCK), F32),
        ],
        compiler_params=pltpu.CompilerParams(
            dimension_semantics=("arbitrary", "arbitrary", "arbitrary"),
            vmem_limit_bytes=VMEM_LIMIT),
        name="attn_prompt",
    )(q, q, k, v)


def _attn_sample_kernel(pt_ref, q_ref, kn_ref, vn_ref, ck_hbm, cv_hbm, o_ref, kbuf, vbuf, sem, *,
                        n_pages, n_new, n_seq):
    i = pl.program_id(0)
    pages_per_blk = MOBA_BLOCK // PAGE_SIZE
    n_blk = n_pages // pages_per_blk

    def page_copies(seq):
        slot = seq % PAGE_SLOTS
        copies = []
        for p in range(n_pages):
            page = pt_ref[seq * n_pages + p]
            copies.append(pltpu.make_async_copy(ck_hbm.at[page], kbuf.at[slot, p], sem.at[0, slot, p]))
            copies.append(pltpu.make_async_copy(cv_hbm.at[page], vbuf.at[slot, p], sem.at[1, slot, p]))
        return copies

    @pl.when(i == 0)
    def _():
        for ahead in range(min(PAGE_SLOTS - 1, n_seq)):
            for c in page_copies(ahead):
                c.start()

    @pl.when(i + PAGE_SLOTS - 1 < n_seq)
    def _():
        for c in page_copies(i + PAGE_SLOTS - 1):
            c.start()

    for c in page_copies(i):
        c.wait()
    slot = i % PAGE_SLOTS
    k_refs = [kbuf.at[slot, p] for p in range(n_pages)]
    v_refs = [vbuf.at[slot, p] for p in range(n_pages)]

    q = q_ref[...]
    r = lax.broadcasted_iota(jnp.int32, q.shape, 0)
    lane = lax.broadcasted_iota(jnp.int32, q.shape, 1)
    diag = (lane // HEAD_DIM) == (r // n_new)
    qbd = jnp.where(diag, q, jnp.zeros_like(q))
    qf = qbd.astype(F32)

    def block_of(refs, j):
        pages = [refs[j * pages_per_blk + i][...].astype(BF16) for i in range(pages_per_blk)]
        return jnp.concatenate(pages, axis=1)

    s_blk = [_dot(qbd, block_of(k_refs, j)) for j in range(n_blk)]
    sb = [jnp.sum(s, axis=-1, keepdims=True) * (1.0 / MOBA_BLOCK) for s in s_blk]

    sel = []
    for j in range(n_blk):
        rank = jnp.zeros(sb[j].shape, jnp.int32)
        for i in range(n_blk):
            if i == j:
                continue
            beats = (sb[i] >= sb[j]) if i < j else (sb[i] > sb[j])
            rank = rank + beats.astype(jnp.int32)
        sel.append(rank < MOBA_TOPK)

    tok = r[:, 0:1] % n_new
    s_past = [jnp.where(sel[j], s_blk[j], MASK_VALUE) for j in range(n_blk)]
    kn = kn_ref[...]
    vn = vn_ref[...]
    s_new = [jnp.where(tok >= i, jnp.sum(qf * kn[i:i + 1, :], axis=-1, keepdims=True), MASK_VALUE)
             for i in range(n_new)]
    m = s_new[0]
    for s in s_new[1:]:
        m = jnp.maximum(m, s)
    for s in s_past:
        m = jnp.maximum(m, jnp.max(s, axis=-1, keepdims=True))

    l = jnp.zeros(m.shape, F32)
    acc = jnp.zeros(q.shape, F32)
    for i in range(n_new):
        p = jnp.exp2(s_new[i] - m)
        l = l + p
        acc = acc + p * vn[i:i + 1, :]
    for j in range(n_blk):
        p = jnp.exp2(s_past[j] - m)
        l = l + jnp.sum(p, axis=-1, keepdims=True)
        acc = acc + _dot_nt(p.astype(BF16), block_of(v_refs, j))
    o = jnp.where(diag, acc / l, 0.0)
    while o.shape[0] > 8:
        half = o.shape[0] // 2
        o = o[:half] + o[half:]
    shift = 4
    while shift >= n_new:
        o = o + pltpu.roll(o, shift, axis=0)
        shift //= 2
    o_ref[...] = o.astype(o_ref.dtype)


def _attn_sample(q, k_new, v_new, cache_k, cache_v, page_table):
    n, t, w = q.shape
    n_pages = page_table.shape[1]
    assert (n_pages * PAGE_SIZE) % MOBA_BLOCK == 0 and MOBA_BLOCK % PAGE_SIZE == 0
    assert t in (1, 2, 4, 8) and n_pages * PAGE_SIZE // MOBA_BLOCK >= MOBA_TOPK
    n_pool = cache_k.shape[0]
    ck = cache_k.transpose(0, 2, 3, 1).reshape(n_pool, w, PAGE_SIZE)
    cv = cache_v.transpose(0, 2, 3, 1).reshape(n_pool, w, PAGE_SIZE)
    q_rep = jnp.tile(q, (1, N_HEADS, 1))
    pad = ((0, 0), (0, 8 - t), (0, 0))
    kn = jnp.pad(k_new, pad)
    vn = jnp.pad(v_new, pad)

    grid_spec = pltpu.PrefetchScalarGridSpec(
        num_scalar_prefetch=1,
        grid=(n,),
        in_specs=[
            pl.BlockSpec((None, N_HEADS * t, w), lambda i, pt: (i, 0, 0)),
            pl.BlockSpec((None, 8, w), lambda i, pt: (i, 0, 0)),
            pl.BlockSpec((None, 8, w), lambda i, pt: (i, 0, 0)),
            pl.BlockSpec(memory_space=pl.ANY),
            pl.BlockSpec(memory_space=pl.ANY),
        ],
        out_specs=pl.BlockSpec((None, 8, w), lambda i, pt: (i, 0, 0)),
        scratch_shapes=[
            pltpu.VMEM((PAGE_SLOTS, n_pages, w, PAGE_SIZE), F32),
            pltpu.VMEM((PAGE_SLOTS, n_pages, w, PAGE_SIZE), F32),
            pltpu.SemaphoreType.DMA((2, PAGE_SLOTS, n_pages)),
        ],
    )
    out = pl.pallas_call(
        functools.partial(_attn_sample_kernel, n_pages=n_pages, n_new=t, n_seq=n),
        grid_spec=grid_spec,
        out_shape=jax.ShapeDtypeStruct((n, 8, w), BF16),
        compiler_params=pltpu.CompilerParams(
            dimension_semantics=("arbitrary",), vmem_limit_bytes=VMEM_LIMIT),
        name="attn_sample",
    )(page_table.reshape(-1), q_rep, kn, vn, ck, cv)
    return out[:, :t]


def _merge_ffn_kernel(x_ref, ao_ref, co_ref, ga_ref, gc_ref, wpa_ref, wpc_ref, wo_ref,
                      g1_ref, g2_ref, g3_ref, wg_ref, wu_ref, wd_ref, y_ref, a_ref, *, ff_chunk):
    m = ga_ref[...] * _dot(ao_ref[...], wpa_ref[...]) + gc_ref[...] * _dot(co_ref[...], wpc_ref[...])
    h = x_ref[...] + _rms(_dot(m.astype(BF16), wo_ref[...])) * g1_ref[...]
    hn = (_rms(h) * g2_ref[...]).astype(BF16)
    d_ff = wg_ref.shape[1]
    for c in range(0, d_ff, ff_chunk):
        g = _dot(hn, wg_ref[:, c:c + ff_chunk])
        u = _dot(hn, wu_ref[:, c:c + ff_chunk])
        a_ref[:, c:c + ff_chunk] = (g * jax.nn.sigmoid(g) * u).astype(BF16)
    f = _dot(a_ref[...], wd_ref[...])
    y_ref[...] = h + _rms(f) * g3_ref[...]


def _merge_ffn(x, ao, co, ga, gc, wpa, wpc, wo, g1, g2, g3, wg, wu, wd, tm):
    rows, d = x.shape
    cw = ao.shape[1]
    d_ff = wg.shape[1]
    ff_chunk = 256
    assert d_ff % ff_chunk == 0 and rows % tm == 0
    row_blk = lambda width: pl.BlockSpec((tm, width), lambda i: (i, 0))
    return pl.pallas_call(
        functools.partial(_merge_ffn_kernel, ff_chunk=ff_chunk),
        grid=(rows // tm,),
        in_specs=[row_blk(d), row_blk(cw), row_blk(cw), row_blk(d), row_blk(d),
                  _resident(wpa.shape), _resident(wpc.shape), _resident(wo.shape),
                  _resident((1, d)), _resident((1, d)), _resident((1, d)),
                  _resident(wg.shape), _resident(wu.shape), _resident(wd.shape)],
        out_specs=row_blk(d),
        out_shape=jax.ShapeDtypeStruct((rows, d), F32),
        scratch_shapes=[pltpu.VMEM((tm, d_ff), BF16)],
        compiler_params=pltpu.CompilerParams(
            dimension_semantics=("arbitrary",), vmem_limit_bytes=VMEM_LIMIT),
        name="merge_ffn",
    )(x, ao, co, ga, gc, wpa, wpc, wo, g1.reshape(1, d), g2.reshape(1, d), g3.reshape(1, d),
      wg, wu, wd)


def kernel(x_prompt, x_sample, cache_k, cache_v, state_conv, page_table, g_attn_pre, w_in, w_conv, w_proj_attn, w_proj_conv, w_out, g_attn_post, g_ffn_pre, w_gate, w_up, w_down, g_ffn_post):
    depth = w_in.shape[0]
    bsz, s, d = x_prompt.shape
    n, t, _ = x_sample.shape
    cw = ATTN_WIDTH
    assert t >= CONV_WIDTH - 1 and s >= CONV_WIDTH - 1
    xp, xs = x_prompt, x_sample
    outs = [[] for _ in range(6)]
    for l in range(depth):
        w_in_b = w_in[l].astype(BF16)
        tail = (w_proj_attn[l].astype(BF16), w_proj_conv[l].astype(BF16), w_out[l].astype(BF16),
                g_attn_post[l], g_ffn_pre[l], g_ffn_post[l],
                w_gate[l].astype(BF16), w_up[l].astype(BF16), w_down[l].astype(BF16))

        prev0 = jnp.zeros((bsz, CONV_WIDTH - 1, cw), F32)
        q, k, v, co, ga, gc, u_tail = _proj_prompt(xp, g_attn_pre[l], w_in_b, w_conv[l], prev0, 512)
        ao = _attn_prompt(q, k, v)
        flat = lambda a: a.reshape(bsz * s, a.shape[-1])
        xp = _merge_ffn(flat(xp), flat(ao), flat(co), flat(ga), flat(gc), *tail, 512)
        xp = xp.reshape(bsz, s, d)
        outs[0].append(k.reshape(bsz, s, N_HEADS, HEAD_DIM))
        outs[1].append(v.reshape(bsz, s, N_HEADS, HEAD_DIM))
        outs[2].append(u_tail[:, 8 - (CONV_WIDTH - 1):])

        q, k, v, co, ga, gc, u = _proj_sample(xs, g_attn_pre[l], w_in_b, w_conv[l], state_conv[l])
        ao = _attn_sample(q.reshape(n, t, cw), k.reshape(n, t, cw), v.reshape(n, t, cw),
                          cache_k[l], cache_v[l], page_table)
        xs = _merge_ffn(xs.reshape(n * t, d), ao.reshape(n * t, cw), co, ga, gc, *tail, 256)
        xs = xs.reshape(n, t, d)
        outs[3].append(k.reshape(n, t, N_HEADS, HEAD_DIM))
        outs[4].append(v.reshape(n, t, N_HEADS, HEAD_DIM))
        outs[5].append(u.reshape(n, t, cw)[:, t - (CONV_WIDTH - 1):])

    kp, vp, cp, ks, vs, cs = (jnp.stack(o) for o in outs)
    return (xp, xs, kp, vp, cp, ks, vs, cs)
```

```python
import functools

import jax
import jax.numpy as jnp
from jax import lax
from jax.experimental import pallas as pl
from jax.experimental.pallas import tpu as pltpu

N_HEADS = 8
HEAD_DIM = 64
ATTN_WIDTH = N_HEADS * HEAD_DIM
MOBA_BLOCK = 256
MOBA_TOPK = 3
CONV_WIDTH = 3
PAGE_SIZE = 128
RMS_EPS = 1e-6

LANES = 128
HEADS_PER_GROUP = LANES // HEAD_DIM
MASK_VALUE = -(2.0 ** 100)
Q_SCALE = HEAD_DIM ** -0.5 * 1.4426950408889634
VMEM_LIMIT = 52 * 1024 * 1024
PAGE_SLOTS = 3

PROJ_ROWS = 512
PROJ_SAMPLE_ROWS = 128
MERGE_ROWS = 512
MERGE_SAMPLE_ROWS = 256
FF_CHUNK = 256
ATTN_CHUNK = 4

F32 = jnp.float32
BF16 = jnp.bfloat16


def _dot(a, b):
    return jnp.dot(a, b, preferred_element_type=F32)


def _dot_nt(a, b):
    return lax.dot_general(a, b, (((1,), (1,)), ((), ())), preferred_element_type=F32)


def _rms(x):
    return x * lax.rsqrt(jnp.mean(x * x, axis=-1, keepdims=True) + RMS_EPS)


def _resident(shape):
    return pl.BlockSpec(shape, lambda *_: (0,) * len(shape), pipeline_mode=pl.Buffered(1))


def _proj_kernel(x_ref, g_ref, w_ref, wc_ref, b1_ref, b2_ref,
                 q_ref, k_ref, v_ref, co_ref, ga_ref, gc_ref, u_ref,
                 carry_ref, *, seq_rows, d_model):
    tm = x_ref.shape[0]
    cw = ATTN_WIDTH
    xn = (_rms(x_ref[...]) * g_ref[...]).astype(BF16)

    def proj(lo, width):
        return _dot(xn, w_ref[:, lo:lo + width])

    ga_ref[...] = jax.nn.sigmoid(proj(6 * cw, d_model))
    gc_ref[...] = jax.nn.sigmoid(proj(6 * cw + d_model, d_model))
    cc = proj(4 * cw, cw)
    cx = proj(5 * cw, cw)
    u = cc * cx
    row = lax.broadcasted_iota(jnp.int32, u.shape, 0)
    s1 = pltpu.roll(u, 1, axis=0)
    s2 = pltpu.roll(u, 2, axis=0)
    if seq_rows is None:
        @pl.when(pl.program_id(1) == 0)
        def _():
            carry_ref[...] = b1_ref[0]
        c = carry_ref[...]
        prev1 = jnp.broadcast_to(c[7:8, :], u.shape)
        prev2 = jnp.broadcast_to(c[6:7, :], u.shape)
        s1 = jnp.where(row == 0, prev1, s1)
        s2 = jnp.where(row == 0, prev2, jnp.where(row == 1, prev1, s2))
        carry_ref[...] = u[tm - 8:tm, :]
        u_ref[0] = u[tm - 8:tm, :]
    else:
        t = row % seq_rows
        s1 = jnp.where(t >= 1, s1, b1_ref[...])
        s2 = jnp.where(t >= 2, s2, b2_ref[...])
        u_ref[...] = u
    wc = wc_ref[...]
    y = wc[0:1, :] * s2 + wc[1:2, :] * s1 + wc[2:3, :] * u
    co_ref[...] = (proj(3 * cw, cw) * y).astype(BF16)
    q_ref[...] = (proj(0, cw) * Q_SCALE).astype(BF16)
    k_ref[...] = proj(cw, cw)
    v_ref[...] = proj(2 * cw, cw)


def _proj_prompt(x, g, w_in, w_conv, prev, tm):
    bsz, s, d = x.shape
    cw = ATTN_WIDTH
    hist = jnp.zeros((bsz, 8, cw), F32).at[:, 8 - (CONV_WIDTH - 1):].set(prev)
    row_blk = lambda width: pl.BlockSpec((None, tm, width), lambda b, i: (b, i, 0))
    outs = pl.pallas_call(
        functools.partial(_proj_kernel, seq_rows=None, d_model=d),
        grid=(bsz, s // tm),
        in_specs=[
            row_blk(d),
            _resident((1, d)),
            _resident(w_in.shape),
            _resident(w_conv.shape),
            pl.BlockSpec((1, 8, cw), lambda b, i: (b, 0, 0)),
            pl.BlockSpec((1, 8, cw), lambda b, i: (b, 0, 0)),
        ],
        out_specs=[row_blk(cw), row_blk(cw), row_blk(cw), row_blk(cw), row_blk(d), row_blk(d),
                   pl.BlockSpec((1, 8, cw), lambda b, i: (b, 0, 0))],
        out_shape=[
            jax.ShapeDtypeStruct((bsz, s, cw), BF16),
            jax.ShapeDtypeStruct((bsz, s, cw), F32),
            jax.ShapeDtypeStruct((bsz, s, cw), F32),
            jax.ShapeDtypeStruct((bsz, s, cw), BF16),
            jax.ShapeDtypeStruct((bsz, s, d), F32),
            jax.ShapeDtypeStruct((bsz, s, d), F32),
            jax.ShapeDtypeStruct((bsz, 8, cw), F32),
        ],
        scratch_shapes=[pltpu.VMEM((8, cw), F32)],
        compiler_params=pltpu.CompilerParams(
            dimension_semantics=("arbitrary", "arbitrary"), vmem_limit_bytes=VMEM_LIMIT),
        name="proj_prompt",
    )(x, g.reshape(1, d), w_in, w_conv, hist, hist)
    return outs


def _proj_sample(x, g, w_in, w_conv, state):
    n, t, d = x.shape
    cw = ATTN_WIDTH
    rows = n * t
    b1 = jnp.zeros((n, t, cw), F32).at[:, 0].set(state[:, 1]).reshape(rows, cw)
    b2 = jnp.zeros((n, t, cw), F32).at[:, 0].set(state[:, 0]).at[:, 1].set(state[:, 1])
    b2 = b2.reshape(rows, cw)
    tm = PROJ_SAMPLE_ROWS
    assert rows % tm == 0 and tm % t == 0
    full = lambda width: pl.BlockSpec((tm, width), lambda i: (i, 0))
    outs = pl.pallas_call(
        functools.partial(_proj_kernel, seq_rows=t, d_model=d),
        grid=(rows // tm,),
        in_specs=[full(d), _resident((1, d)), _resident(w_in.shape), _resident(w_conv.shape),
                  full(cw), full(cw)],
        out_specs=[full(cw), full(cw), full(cw), full(cw), full(d), full(d), full(cw)],
        out_shape=[
            jax.ShapeDtypeStruct((rows, cw), BF16),
            jax.ShapeDtypeStruct((rows, cw), F32),
            jax.ShapeDtypeStruct((rows, cw), F32),
            jax.ShapeDtypeStruct((rows, cw), BF16),
            jax.ShapeDtypeStruct((rows, d), F32),
            jax.ShapeDtypeStruct((rows, d), F32),
            jax.ShapeDtypeStruct((rows, cw), F32),
        ],
        scratch_shapes=[pltpu.VMEM((8, cw), F32)],
        compiler_params=pltpu.CompilerParams(
            dimension_semantics=("arbitrary",), vmem_limit_bytes=VMEM_LIMIT),
        name="proj_sample",
    )(x.reshape(rows, d), g.reshape(1, d), w_in, w_conv, b1, b2)
    return outs


V_ROWS = HEAD_DIM + 16


def _select_blocks(scores, n_valid):
    blk_id = lax.broadcasted_iota(jnp.int32, scores.shape, 0)
    blk_f = blk_id.astype(F32)
    cand = jnp.where(blk_id < n_valid, scores, -jnp.inf)
    sel = jnp.zeros(scores.shape, jnp.bool_)
    for _ in range(MOBA_TOPK):
        mx = jnp.max(cand, axis=0, keepdims=True)
        idx = jnp.min(jnp.where(cand == mx, blk_f, float(scores.shape[0])), axis=0, keepdims=True)
        pick = (blk_f == idx) & (mx > -jnp.inf)
        sel = sel | pick
        cand = jnp.where(pick, -jnp.inf, cand)
    return sel


def _attn_prompt_kernel(q_ref, qn_ref, k_ref, v_ref, o_ref, kaug_ref, vt_ref, kmean_ref, qaug_ref,
                        s_ref, *, n_blocks, chunk):
    t = pl.program_id(2)
    blk = MOBA_BLOCK
    heads = range(HEADS_PER_GROUP)

    def put_query(q_rows_ref, n_valid, slot):
        q_t = q_rows_ref[...].astype(F32).T
        d_row = lax.broadcasted_iota(jnp.int32, q_t.shape, 0)
        kmean = kmean_ref[...].astype(BF16)
        pad_rows = jnp.zeros((LANES - n_blocks, blk), BF16)
        for h in heads:
            in_head = (d_row >= h * HEAD_DIM) & (d_row < (h + 1) * HEAD_DIM)
            qh = jnp.where(in_head, q_t, 0.0).astype(BF16)
            pen = jnp.where(_select_blocks(_dot(kmean, qh), n_valid), 0.0, MASK_VALUE).astype(BF16)
            qaug_ref[slot, h] = jnp.concatenate([qh, pen, pad_rows], axis=0)

    @pl.when(t == 0)
    def _init():
        lane_k = lax.broadcasted_iota(jnp.int32, (blk, LANES), 1)
        pad_i = lax.broadcasted_iota(jnp.int32, (V_ROWS - HEAD_DIM, blk), 0)
        ones_rows = (pad_i == 0).astype(BF16)

        def fill(j, c):
            rows = pl.ds(pl.multiple_of(j * blk, blk), blk)
            kb = k_ref[rows, :]
            kaug_ref[rows, 0:LANES] = kb.astype(BF16)
            kaug_ref[rows, LANES:2 * LANES] = (lane_k == j).astype(BF16)
            v_t = v_ref[rows, :].T
            for h in heads:
                vt_ref[j, h, 0:HEAD_DIM] = v_t[h * HEAD_DIM:(h + 1) * HEAD_DIM, :].astype(BF16)
                vt_ref[j, h, HEAD_DIM:V_ROWS] = ones_rows
            kmean_ref[pl.ds(j, 1), :] = jnp.sum(kb, axis=0, keepdims=True) * (1.0 / blk)
            return c

        lax.fori_loop(0, n_blocks, fill, 0)
        put_query(q_ref, 0, 0)

    cur = t % 2
    q_aug = [qaug_ref[cur, h] for h in heads]
    key_i = lax.broadcasted_iota(jnp.int32, (blk, blk), 0)
    qry_i = lax.broadcasted_iota(jnp.int32, (blk, blk), 1)
    causal = key_i <= qry_i
    own = pl.ds(pl.multiple_of(t * blk, blk), blk)
    k_own = kaug_ref[own, 0:LANES]
    last_chunk = n_blocks // chunk - 1

    def put_scores(c, slot):
        rows = pl.ds(pl.multiple_of(c * (chunk * blk), chunk * blk), chunk * blk)
        kc = kaug_ref[rows, :]
        for h in heads:
            s_ref[slot, h] = _dot(kc, q_aug[h])

    def consume(c, slot, state):
        new = []
        for h in heads:
            m, acc = state[h]
            s = s_ref[slot, h]
            m_new = jnp.maximum(m, jnp.max(s, axis=0, keepdims=True))
            p = jnp.exp2(s - m_new).astype(BF16)
            acc = jnp.exp2(m - m_new) * acc
            for i in range(chunk):
                acc = acc + _dot(vt_ref[c * chunk + i, h], p[i * blk:(i + 1) * blk, :])
            new.append((m_new, acc))
        return tuple(new)

    own_scores = [_dot(k_own, q_aug[h][0:LANES]) for h in heads]
    put_scores(0, 0)
    state = []
    for h in heads:
        s = jnp.where(causal, own_scores[h], MASK_VALUE)
        m = jnp.max(s, axis=0, keepdims=True)
        state.append((m, _dot(vt_ref[t, h], jnp.exp2(s - m).astype(BF16))))
    put_query(qn_ref, t + 1, 1 - cur)

    n_chunks = (t + chunk - 1) // chunk

    def step_pair(j, state):
        put_scores(2 * j + 1, 1)
        state = consume(2 * j, 0, state)
        put_scores(jnp.minimum(2 * j + 2, last_chunk), 0)
        return consume(2 * j + 1, 1, state)

    def step_quad(j, state):
        return step_pair(2 * j + 1, step_pair(2 * j, state))

    n_pairs = n_chunks // 2
    state = lax.fori_loop(0, n_pairs // 2, step_quad, tuple(state))
    state = lax.fori_loop(n_pairs // 2 * 2, n_pairs, step_pair, state)
    fin = lax.cond(n_chunks % 2 == 1, lambda st: consume(n_chunks - 1, 0, st), lambda st: st, state)
    o_t = jnp.concatenate([acc[0:HEAD_DIM] * (1.0 / acc[HEAD_DIM:HEAD_DIM + 1]) for (_, acc) in fin],
                          axis=0)
    o_ref[...] = o_t.T.astype(o_ref.dtype)


def _attn_prompt(q, k, v):
    bsz, s, w = q.shape
    n_blocks = s // MOBA_BLOCK
    chunk = ATTN_CHUNK
    assert n_blocks % (2 * chunk) == 0 and n_blocks % 16 == 0 and n_blocks <= LANES
    groups = w // LANES
    q_blk = pl.BlockSpec((None, MOBA_BLOCK, LANES), lambda b, g, t: (b, t, g))
    q_next = pl.BlockSpec((None, MOBA_BLOCK, LANES),
                          lambda b, g, t: (b, jnp.minimum(t + 1, n_blocks - 1), g))
    kv_blk = pl.BlockSpec((None, s, LANES), lambda b, g, t: (b, 0, g))
    return pl.pallas_call(
        functools.partial(_attn_prompt_kernel, n_blocks=n_blocks, chunk=chunk),
        grid=(bsz, groups, n_blocks),
        in_specs=[q_blk, q_next, kv_blk, kv_blk],
        out_specs=q_blk,
        out_shape=jax.ShapeDtypeStruct((bsz, s, w), BF16),
        scratch_shapes=[
            pltpu.VMEM((s, 2 * LANES), BF16),
            pltpu.VMEM((n_blocks, HEADS_PER_GROUP, V_ROWS, MOBA_BLOCK), BF16),
            pltpu.VMEM((n_blocks, LANES), F32),
            pltpu.VMEM((2, HEADS_PER_GROUP, 2 * LANES, MOBA_BLOCK), BF16),
            pltpu.VMEM((2, HEADS_PER_GROUP, chunk * MOBA_BLOCK, MOBA_BLOCK), F32),
        ],
        compiler_params=pltpu.CompilerParams(
            dimension_semantics=("arbitrary", "arbitrary", "arbitrary"),
            vmem_limit_bytes=VMEM_LIMIT),
        name="attn_prompt",
    )(q, q, k, v)


def _attn_sample_kernel(pt_ref, q_ref, kn_ref, vn_ref, ck_hbm, cv_hbm, o_ref, kbuf, vbuf, sem, *,
                        n_pages, n_new, n_seq):
    i = pl.program_id(0)
    pages_per_blk = MOBA_BLOCK // PAGE_SIZE
    n_blk = n_pages // pages_per_blk

    def page_copies(seq):
        slot = seq % PAGE_SLOTS
        copies = []
        for p in range(n_pages):
            page = pt_ref[seq * n_pages + p]
            copies.append(pltpu.make_async_copy(ck_hbm.at[page], kbuf.at[slot, p], sem.at[0, slot, p]))
            copies.append(pltpu.make_async_copy(cv_hbm.at[page], vbuf.at[slot, p], sem.at[1, slot, p]))
        return copies

    @pl.when(i == 0)
    def _():
        for ahead in range(min(PAGE_SLOTS - 1, n_seq)):
            for c in page_copies(ahead):
                c.start()

    @pl.when(i + PAGE_SLOTS - 1 < n_seq)
    def _():
        for c in page_copies(i + PAGE_SLOTS - 1):
            c.start()

    for c in page_copies(i):
        c.wait()
    slot = i % PAGE_SLOTS
    k_refs = [kbuf.at[slot, p] for p in range(n_pages)]
    v_refs = [vbuf.at[slot, p] for p in range(n_pages)]

    q = q_ref[...]
    r = lax.broadcasted_iota(jnp.int32, q.shape, 0)
    lane = lax.broadcasted_iota(jnp.int32, q.shape, 1)
    diag = (lane // HEAD_DIM) == (r // n_new)
    qbd = jnp.where(diag, q, jnp.zeros_like(q))
    qf = qbd.astype(F32)

    def block_of(refs, j):
        pages = [refs[j * pages_per_blk + i][...].astype(BF16) for i in range(pages_per_blk)]
        return jnp.concatenate(pages, axis=1)

    s_blk = [_dot(qbd, block_of(k_refs, j)) for j in range(n_blk)]
    sb = [jnp.sum(s, axis=-1, keepdims=True) * (1.0 / MOBA_BLOCK) for s in s_blk]

    sel = []
    for j in range(n_blk):
        rank = jnp.zeros(sb[j].shape, jnp.int32)
        for i in range(n_blk):
            if i == j:
                continue
            beats = (sb[i] >= sb[j]) if i < j else (sb[i] > sb[j])
            rank = rank + beats.astype(jnp.int32)
        sel.append(rank < MOBA_TOPK)

    tok = r[:, 0:1] % n_new
    s_past = [jnp.where(sel[j], s_blk[j], MASK_VALUE) for j in range(n_blk)]
    kn = kn_ref[...]
    vn = vn_ref[...]
    s_new = [jnp.where(tok >= i, jnp.sum(qf * kn[i:i + 1, :], axis=-1, keepdims=True), MASK_VALUE)
             for i in range(n_new)]
    m = s_new[0]
    for s in s_new[1:]:
        m = jnp.maximum(m, s)
    for s in s_past:
        m = jnp.maximum(m, jnp.max(s, axis=-1, keepdims=True))

    l = jnp.zeros(m.shape, F32)
    acc = jnp.zeros(q.shape, F32)
    for i in range(n_new):
        p = jnp.exp2(s_new[i] - m)
        l = l + p
        acc = acc + p * vn[i:i + 1, :]
    for j in range(n_blk):
        p = jnp.exp2(s_past[j] - m)
        l = l + jnp.sum(p, axis=-1, keepdims=True)
        acc = acc + _dot_nt(p.astype(BF16), block_of(v_refs, j))
    o = jnp.where(diag, acc / l, 0.0)
    while o.shape[0] > 8:
        half = o.shape[0] // 2
        o = o[:half] + o[half:]
    shift = 4
    while shift >= n_new:
        o = o + pltpu.roll(o, shift, axis=0)
        shift //= 2
    o_ref[...] = o.astype(o_ref.dtype)


def _attn_sample(q, k_new, v_new, cache_k, cache_v, page_table):
    n, t, w = q.shape
    n_pages = page_table.shape[1]
    assert (n_pages * PAGE_SIZE) % MOBA_BLOCK == 0 and MOBA_BLOCK % PAGE_SIZE == 0
    assert t in (1, 2, 4, 8) and n_pages * PAGE_SIZE // MOBA_BLOCK >= MOBA_TOPK
    n_pool = cache_k.shape[0]
    ck = cache_k.transpose(0, 2, 3, 1).reshape(n_pool, w, PAGE_SIZE)
    cv = cache_v.transpose(0, 2, 3, 1).reshape(n_pool, w, PAGE_SIZE)
    q_rep = jnp.tile(q, (1, N_HEADS, 1))

    grid_spec = pltpu.PrefetchScalarGridSpec(
        num_scalar_prefetch=1,
        grid=(n,),
        in_specs=[
            pl.BlockSpec((None, N_HEADS * t, w), lambda i, pt: (i, 0, 0)),
            pl.BlockSpec((None, t, w), lambda i, pt: (i, 0, 0)),
            pl.BlockSpec((None, t, w), lambda i, pt: (i, 0, 0)),
            pl.BlockSpec(memory_space=pl.ANY),
            pl.BlockSpec(memory_space=pl.ANY),
        ],
        out_specs=pl.BlockSpec((None, 8, w), lambda i, pt: (i, 0, 0)),
        scratch_shapes=[
            pltpu.VMEM((PAGE_SLOTS, n_pages, w, PAGE_SIZE), F32),
            pltpu.VMEM((PAGE_SLOTS, n_pages, w, PAGE_SIZE), F32),
            pltpu.SemaphoreType.DMA((2, PAGE_SLOTS, n_pages)),
        ],
    )
    out = pl.pallas_call(
        functools.partial(_attn_sample_kernel, n_pages=n_pages, n_new=t, n_seq=n),
        grid_spec=grid_spec,
        out_shape=jax.ShapeDtypeStruct((n, 8, w), BF16),
        compiler_params=pltpu.CompilerParams(
            dimension_semantics=("arbitrary",), vmem_limit_bytes=VMEM_LIMIT),
        name="attn_sample",
    )(page_table.reshape(-1), q_rep, k_new, v_new, ck, cv)
    return out[:, :t]


def _merge_ffn_kernel(x_ref, ao_ref, co_ref, ga_ref, gc_ref, wpa_ref, wpc_ref, wo_ref,
                      g1_ref, g2_ref, g3_ref, wg_ref, wu_ref, wd_ref, y_ref, a_ref, *, ff_chunk):
    m = ga_ref[...] * _dot(ao_ref[...], wpa_ref[...]) + gc_ref[...] * _dot(co_ref[...], wpc_ref[...])
    h = x_ref[...] + _rms(_dot(m.astype(BF16), wo_ref[...])) * g1_ref[...]
    hn = (_rms(h) * g2_ref[...]).astype(BF16)
    d_ff = wg_ref.shape[1]
    for c in range(0, d_ff, ff_chunk):
        g = _dot(hn, wg_ref[:, c:c + ff_chunk])
        u = _dot(hn, wu_ref[:, c:c + ff_chunk])
        a_ref[:, c:c + ff_chunk] = (g * jax.nn.sigmoid(g) * u).astype(BF16)
    f = _dot(a_ref[...], wd_ref[...])
    y_ref[...] = h + _rms(f) * g3_ref[...]


def _merge_ffn(x, ao, co, ga, gc, wpa, wpc, wo, g1, g2, g3, wg, wu, wd, tm):
    rows, d = x.shape
    cw = ao.shape[1]
    d_ff = wg.shape[1]
    ff_chunk = FF_CHUNK
    assert d_ff % ff_chunk == 0 and rows % tm == 0
    row_blk = lambda width: pl.BlockSpec((tm, width), lambda i: (i, 0))
    return pl.pallas_call(
        functools.partial(_merge_ffn_kernel, ff_chunk=ff_chunk),
        grid=(rows // tm,),
        in_specs=[row_blk(d), row_blk(cw), row_blk(cw), row_blk(d), row_blk(d),
                  _resident(wpa.shape), _resident(wpc.shape), _resident(wo.shape),
                  _resident((1, d)), _resident((1, d)), _resident((1, d)),
                  _resident(wg.shape), _resident(wu.shape), _resident(wd.shape)],
        out_specs=row_blk(d),
        out_shape=jax.ShapeDtypeStruct((rows, d), F32),
        scratch_shapes=[pltpu.VMEM((tm, d_ff), BF16)],
        compiler_params=pltpu.CompilerParams(
            dimension_semantics=("arbitrary",), vmem_limit_bytes=VMEM_LIMIT),
        name="merge_ffn",
    )(x, ao, co, ga, gc, wpa, wpc, wo, g1.reshape(1, d), g2.reshape(1, d), g3.reshape(1, d),
      wg, wu, wd)


def kernel(x_prompt, x_sample, cache_k, cache_v, state_conv, page_table, g_attn_pre, w_in, w_conv, w_proj_attn, w_proj_conv, w_out, g_attn_post, g_ffn_pre, w_gate, w_up, w_down, g_ffn_post):
    depth = w_in.shape[0]
    bsz, s, d = x_prompt.shape
    n, t, _ = x_sample.shape
    cw = ATTN_WIDTH
    assert t >= CONV_WIDTH - 1 and s >= CONV_WIDTH - 1
    xp, xs = x_prompt, x_sample
    outs = [[] for _ in range(6)]
    for l in range(depth):
        w_in_b = w_in[l].astype(BF16)
        tail = (w_proj_attn[l].astype(BF16), w_proj_conv[l].astype(BF16), w_out[l].astype(BF16),
                g_attn_post[l], g_ffn_pre[l], g_ffn_post[l],
                w_gate[l].astype(BF16), w_up[l].astype(BF16), w_down[l].astype(BF16))

        prev0 = jnp.zeros((bsz, CONV_WIDTH - 1, cw), F32)
        q, k, v, co, ga, gc, u_tail = _proj_prompt(xp, g_attn_pre[l], w_in_b, w_conv[l], prev0,
                                                      PROJ_ROWS)
        ao = _attn_prompt(q, k, v)
        flat = lambda a: a.reshape(bsz * s, a.shape[-1])
        xp = _merge_ffn(flat(xp), flat(ao), flat(co), flat(ga), flat(gc), *tail, MERGE_ROWS)
        xp = xp.reshape(bsz, s, d)
        outs[0].append(k.reshape(bsz, s, N_HEADS, HEAD_DIM))
        outs[1].append(v.reshape(bsz, s, N_HEADS, HEAD_DIM))
        outs[2].append(u_tail[:, 8 - (CONV_WIDTH - 1):])

        q, k, v, co, ga, gc, u = _proj_sample(xs, g_attn_pre[l], w_in_b, w_conv[l], state_conv[l])
        ao = _attn_sample(q.reshape(n, t, cw), k.reshape(n, t, cw), v.reshape(n, t, cw),
                          cache_k[l], cache_v[l], page_table)
        xs = _merge_ffn(xs.reshape(n * t, d), ao.reshape(n * t, cw), co, ga, gc, *tail,
                        MERGE_SAMPLE_ROWS)
        xs = xs.reshape(n, t, d)
        outs[3].append(k.reshape(n, t, N_HEADS, HEAD_DIM))
        outs[4].append(v.reshape(n, t, N_HEADS, HEAD_DIM))
        outs[5].append(u.reshape(n, t, cw)[:, t - (CONV_WIDTH - 1):])

    kp, vp, cp, ks, vs, cs = (jnp.stack(o) for o in outs)
    return (xp, xs, kp, vp, cp, ks, vs, cs)
```

```python
import functools

import jax
import jax.numpy as jnp
from jax import lax
from jax.experimental import pallas as pl
from jax.experimental.pallas import tpu as pltpu

N_HEADS = 8
HEAD_DIM = 64
ATTN_WIDTH = N_HEADS * HEAD_DIM
MOBA_BLOCK = 256
MOBA_TOPK = 3
CONV_WIDTH = 3
PAGE_SIZE = 128
RMS_EPS = 1e-6

LANES = 128
HEADS_PER_GROUP = LANES // HEAD_DIM
MASK_VALUE = -(2.0 ** 100)
Q_SCALE = HEAD_DIM ** -0.5 * 1.4426950408889634
VMEM_LIMIT = 52 * 1024 * 1024
PAGE_SLOTS = 3

PROJ_ROWS = 512
PROJ_SAMPLE_ROWS = 128
MERGE_ROWS = 512
MERGE_SAMPLE_ROWS = 256
FF_CHUNK = 256
ATTN_CHUNK = 4

F32 = jnp.float32
BF16 = jnp.bfloat16


def _dot(a, b):
    return jnp.dot(a, b, preferred_element_type=F32)


def _dot_nt(a, b):
    return lax.dot_general(a, b, (((1,), (1,)), ((), ())), preferred_element_type=F32)


def _rms(x):
    return x * lax.rsqrt(jnp.mean(x * x, axis=-1, keepdims=True) + RMS_EPS)


def _resident(shape):
    return pl.BlockSpec(shape, lambda *_: (0,) * len(shape), pipeline_mode=pl.Buffered(1))


def _proj_kernel(x_ref, g_ref, w_ref, wc_ref, b1_ref, b2_ref,
                 q_ref, k_ref, v_ref, co_ref, ga_ref, gc_ref, u_ref,
                 carry_ref, *, seq_rows, d_model):
    tm = x_ref.shape[0]
    cw = ATTN_WIDTH
    xn = (_rms(x_ref[...]) * g_ref[...]).astype(BF16)

    def proj(lo, width):
        return _dot(xn, w_ref[:, lo:lo + width])

    ga_ref[...] = jax.nn.sigmoid(proj(6 * cw, d_model))
    gc_ref[...] = jax.nn.sigmoid(proj(6 * cw + d_model, d_model))
    cc = proj(4 * cw, cw)
    cx = proj(5 * cw, cw)
    u = cc * cx
    row = lax.broadcasted_iota(jnp.int32, u.shape, 0)
    s1 = pltpu.roll(u, 1, axis=0)
    s2 = pltpu.roll(u, 2, axis=0)
    if seq_rows is None:
        @pl.when(pl.program_id(1) == 0)
        def _():
            carry_ref[...] = b1_ref[0]
        c = carry_ref[...]
        prev1 = jnp.broadcast_to(c[7:8, :], u.shape)
        prev2 = jnp.broadcast_to(c[6:7, :], u.shape)
        s1 = jnp.where(row == 0, prev1, s1)
        s2 = jnp.where(row == 0, prev2, jnp.where(row == 1, prev1, s2))
        carry_ref[...] = u[tm - 8:tm, :]
        u_ref[0] = u[tm - 8:tm, :]
    else:
        t = row % seq_rows
        s1 = jnp.where(t >= 1, s1, b1_ref[...])
        s2 = jnp.where(t >= 2, s2, b2_ref[...])
        u_ref[...] = u
    wc = wc_ref[...]
    y = wc[0:1, :] * s2 + wc[1:2, :] * s1 + wc[2:3, :] * u
    co_ref[...] = (proj(3 * cw, cw) * y).astype(BF16)
    q_ref[...] = (proj(0, cw) * Q_SCALE).astype(BF16)
    k_ref[...] = proj(cw, cw)
    v_ref[...] = proj(2 * cw, cw)


def _proj_prompt(x, g, w_in, w_conv, prev, tm):
    bsz, s, d = x.shape
    cw = ATTN_WIDTH
    hist = jnp.zeros((bsz, 8, cw), F32).at[:, 8 - (CONV_WIDTH - 1):].set(prev)
    row_blk = lambda width: pl.BlockSpec((None, tm, width), lambda b, i: (b, i, 0))
    outs = pl.pallas_call(
        functools.partial(_proj_kernel, seq_rows=None, d_model=d),
        grid=(bsz, s // tm),
        in_specs=[
            row_blk(d),
            _resident((1, d)),
            _resident(w_in.shape),
            _resident(w_conv.shape),
            pl.BlockSpec((1, 8, cw), lambda b, i: (b, 0, 0)),
            pl.BlockSpec((1, 8, cw), lambda b, i: (b, 0, 0)),
        ],
        out_specs=[row_blk(cw), row_blk(cw), row_blk(cw), row_blk(cw), row_blk(d), row_blk(d),
                   pl.BlockSpec((1, 8, cw), lambda b, i: (b, 0, 0))],
        out_shape=[
            jax.ShapeDtypeStruct((bsz, s, cw), BF16),
            jax.ShapeDtypeStruct((bsz, s, cw), F32),
            jax.ShapeDtypeStruct((bsz, s, cw), F32),
            jax.ShapeDtypeStruct((bsz, s, cw), BF16),
            jax.ShapeDtypeStruct((bsz, s, d), F32),
            jax.ShapeDtypeStruct((bsz, s, d), F32),
            jax.ShapeDtypeStruct((bsz, 8, cw), F32),
        ],
        scratch_shapes=[pltpu.VMEM((8, cw), F32)],
        compiler_params=pltpu.CompilerParams(
            dimension_semantics=("arbitrary", "arbitrary"), vmem_limit_bytes=VMEM_LIMIT),
        name="proj_prompt",
    )(x, g.reshape(1, d), w_in, w_conv, hist, hist)
    return outs


def _proj_sample(x, g, w_in, w_conv, state):
    n, t, d = x.shape
    cw = ATTN_WIDTH
    rows = n * t
    b1 = jnp.zeros((n, t, cw), F32).at[:, 0].set(state[:, 1]).reshape(rows, cw)
    b2 = jnp.zeros((n, t, cw), F32).at[:, 0].set(state[:, 0]).at[:, 1].set(state[:, 1])
    b2 = b2.reshape(rows, cw)
    tm = PROJ_SAMPLE_ROWS
    assert rows % tm == 0 and tm % t == 0
    full = lambda width: pl.BlockSpec((tm, width), lambda i: (i, 0))
    outs = pl.pallas_call(
        functools.partial(_proj_kernel, seq_rows=t, d_model=d),
        grid=(rows // tm,),
        in_specs=[full(d), _resident((1, d)), _resident(w_in.shape), _resident(w_conv.shape),
                  full(cw), full(cw)],
        out_specs=[full(cw), full(cw), full(cw), full(cw), full(d), full(d), full(cw)],
        out_shape=[
            jax.ShapeDtypeStruct((rows, cw), BF16),
            jax.ShapeDtypeStruct((rows, cw), F32),
            jax.ShapeDtypeStruct((rows, cw), F32),
            jax.ShapeDtypeStruct((rows, cw), BF16),
            jax.ShapeDtypeStruct((rows, d), F32),
            jax.ShapeDtypeStruct((rows, d), F32),
            jax.ShapeDtypeStruct((rows, cw), F32),
        ],
        scratch_shapes=[pltpu.VMEM((8, cw), F32)],
        compiler_params=pltpu.CompilerParams(
            dimension_semantics=("arbitrary",), vmem_limit_bytes=VMEM_LIMIT),
        name="proj_sample",
    )(x.reshape(rows, d), g.reshape(1, d), w_in, w_conv, b1, b2)
    return outs


V_ROWS = HEAD_DIM + 16


def _select_blocks(scores, n_valid):
    blk_id = lax.broadcasted_iota(jnp.int32, scores.shape, 0)
    blk_f = blk_id.astype(F32)
    cand = jnp.where(blk_id < n_valid, scores, -jnp.inf)
    sel = jnp.zeros(scores.shape, jnp.bool_)
    for _ in range(MOBA_TOPK):
        mx = jnp.max(cand, axis=0, keepdims=True)
        idx = jnp.min(jnp.where(cand == mx, blk_f, float(scores.shape[0])), axis=0, keepdims=True)
        pick = (blk_f == idx) & (mx > -jnp.inf)
        sel = sel | pick
        cand = jnp.where(pick, -jnp.inf, cand)
    return sel


SEQ_PAIR = 2


def _attn_prompt_kernel(q_ref, qn_ref, k_ref, v_ref, o_ref, kaug_ref, vt_ref, kmean_ref, qaug_ref,
                        s_ref, *, n_blocks, chunk):
    t = pl.program_id(2)
    blk = MOBA_BLOCK
    heads = range(HEADS_PER_GROUP)
    A, B = range(SEQ_PAIR)

    def put_query(a, q_rows_ref, n_valid, slot):
        q_t = q_rows_ref[a].astype(F32).T
        d_row = lax.broadcasted_iota(jnp.int32, q_t.shape, 0)
        kmean = kmean_ref[a].astype(BF16)
        pad_rows = jnp.zeros((LANES - n_blocks, blk), BF16)
        for h in heads:
            in_head = (d_row >= h * HEAD_DIM) & (d_row < (h + 1) * HEAD_DIM)
            qh = jnp.where(in_head, q_t, 0.0).astype(BF16)
            pen = jnp.where(_select_blocks(_dot(kmean, qh), n_valid), 0.0, MASK_VALUE).astype(BF16)
            qaug_ref[a, slot, h] = jnp.concatenate([qh, pen, pad_rows], axis=0)

    @pl.when(t == 0)
    def _init():
        lane_k = lax.broadcasted_iota(jnp.int32, (blk, LANES), 1)
        pad_i = lax.broadcasted_iota(jnp.int32, (V_ROWS - HEAD_DIM, blk), 0)
        ones_rows = (pad_i == 0).astype(BF16)

        def fill(j, c):
            rows = pl.ds(pl.multiple_of(j * blk, blk), blk)
            for a in (A, B):
                kb = k_ref[a, rows, :]
                kaug_ref[a, rows, 0:LANES] = kb.astype(BF16)
                kaug_ref[a, rows, LANES:2 * LANES] = (lane_k == j).astype(BF16)
                v_t = v_ref[a, rows, :].T
                for h in heads:
                    vt_ref[a, j, h, 0:HEAD_DIM] = v_t[h * HEAD_DIM:(h + 1) * HEAD_DIM, :].astype(BF16)
                    vt_ref[a, j, h, HEAD_DIM:V_ROWS] = ones_rows
                kmean_ref[a, pl.ds(j, 1), :] = jnp.sum(kb, axis=0, keepdims=True) * (1.0 / blk)
            return c

        lax.fori_loop(0, n_blocks, fill, 0)
        for a in (A, B):
            put_query(a, q_ref, 0, 0)

    cur = t % 2
    q_aug = [[qaug_ref[a, cur, h] for h in heads] for a in (A, B)]
    key_i = lax.broadcasted_iota(jnp.int32, (blk, blk), 0)
    qry_i = lax.broadcasted_iota(jnp.int32, (blk, blk), 1)
    causal = key_i <= qry_i
    own = pl.ds(pl.multiple_of(t * blk, blk), blk)
    last_chunk = n_blocks // chunk - 1

    def put_scores(a, c, slot):
        rows = pl.ds(pl.multiple_of(c * (chunk * blk), chunk * blk), chunk * blk)
        kc = kaug_ref[a, rows, :]
        for h in heads:
            s_ref[a, slot, h] = _dot(kc, q_aug[a][h])

    def consume(a, c, slot, state):
        new = []
        for h in heads:
            m, acc = state[h]
            s = s_ref[a, slot, h]
            m_new = jnp.maximum(m, jnp.max(s, axis=0, keepdims=True))
            p = jnp.exp2(s - m_new).astype(BF16)
            acc = jnp.exp2(m - m_new) * acc
            for i in range(chunk):
                acc = acc + _dot(vt_ref[a, c * chunk + i, h], p[i * blk:(i + 1) * blk, :])
            new.append((m_new, acc))
        return tuple(new)

    own_scores = [[_dot(kaug_ref[a, own, 0:LANES], q_aug[a][h][0:LANES]) for h in heads]
                  for a in (A, B)]
    put_scores(A, 0, 0)
    state = []
    for a in (A, B):
        st = []
        for h in heads:
            s = jnp.where(causal, own_scores[a][h], MASK_VALUE)
            m = jnp.max(s, axis=0, keepdims=True)
            st.append((m, _dot(vt_ref[a, t, h], jnp.exp2(s - m).astype(BF16))))
        state.append(tuple(st))
    for a in (A, B):
        put_query(a, qn_ref, t + 1, 1 - cur)

    n_chunks = (t + chunk - 1) // chunk

    def step_pair(j, state):
        st_a, st_b = state
        put_scores(B, 2 * j, 0)
        st_a = consume(A, 2 * j, 0, st_a)
        put_scores(A, 2 * j + 1, 1)
        st_b = consume(B, 2 * j, 0, st_b)
        put_scores(B, 2 * j + 1, 1)
        st_a = consume(A, 2 * j + 1, 1, st_a)
        put_scores(A, jnp.minimum(2 * j + 2, last_chunk), 0)
        st_b = consume(B, 2 * j + 1, 1, st_b)
        return st_a, st_b

    def odd_tail(state):
        st_a, st_b = state
        put_scores(B, n_chunks - 1, 0)
        st_a = consume(A, n_chunks - 1, 0, st_a)
        return st_a, consume(B, n_chunks - 1, 0, st_b)

    state = lax.fori_loop(0, n_chunks // 2, step_pair, tuple(state))
    fin = lax.cond(n_chunks % 2 == 1, odd_tail, lambda st: st, state)
    for a in (A, B):
        o_t = jnp.concatenate(
            [acc[0:HEAD_DIM] * (1.0 / acc[HEAD_DIM:HEAD_DIM + 1]) for (_, acc) in fin[a]], axis=0)
        o_ref[a] = o_t.T.astype(o_ref.dtype)


def _attn_prompt(q, k, v):
    bsz, s, w = q.shape
    n_blocks = s // MOBA_BLOCK
    chunk = ATTN_CHUNK
    assert n_blocks % (2 * chunk) == 0 and n_blocks % 16 == 0 and n_blocks <= LANES
    assert bsz % SEQ_PAIR == 0
    groups = w // LANES
    q_blk = pl.BlockSpec((SEQ_PAIR, MOBA_BLOCK, LANES), lambda b, g, t: (b, t, g))
    q_next = pl.BlockSpec((SEQ_PAIR, MOBA_BLOCK, LANES),
                          lambda b, g, t: (b, jnp.minimum(t + 1, n_blocks - 1), g))
    kv_blk = pl.BlockSpec((SEQ_PAIR, s, LANES), lambda b, g, t: (b, 0, g),
                          pipeline_mode=pl.Buffered(1))
    return pl.pallas_call(
        functools.partial(_attn_prompt_kernel, n_blocks=n_blocks, chunk=chunk),
        grid=(bsz // SEQ_PAIR, groups, n_blocks),
        in_specs=[q_blk, q_next, kv_blk, kv_blk],
        out_specs=q_blk,
        out_shape=jax.ShapeDtypeStruct((bsz, s, w), BF16),
        scratch_shapes=[
            pltpu.VMEM((SEQ_PAIR, s, 2 * LANES), BF16),
            pltpu.VMEM((SEQ_PAIR, n_blocks, HEADS_PER_GROUP, V_ROWS, MOBA_BLOCK), BF16),
            pltpu.VMEM((SEQ_PAIR, n_blocks, LANES), F32),
            pltpu.VMEM((SEQ_PAIR, 2, HEADS_PER_GROUP, 2 * LANES, MOBA_BLOCK), BF16),
            pltpu.VMEM((SEQ_PAIR, 2, HEADS_PER_GROUP, chunk * MOBA_BLOCK, MOBA_BLOCK), F32),
        ],
        compiler_params=pltpu.CompilerParams(
            dimension_semantics=("arbitrary", "arbitrary", "arbitrary"),
            vmem_limit_bytes=VMEM_LIMIT),
        name="attn_prompt",
    )(q, q, k, v)


def _attn_sample_kernel(pt_ref, q_ref, kn_ref, vn_ref, ck_hbm, cv_hbm, o_ref, kbuf, vbuf, sem, *,
                        n_pages, n_new, n_seq):
    i = pl.program_id(0)
    pages_per_blk = MOBA_BLOCK // PAGE_SIZE
    n_blk = n_pages // pages_per_blk

    def page_copies(seq):
        slot = seq % PAGE_SLOTS
        copies = []
        for p in range(n_pages):
            page = pt_ref[seq * n_pages + p]
            copies.append(pltpu.make_async_copy(ck_hbm.at[page], kbuf.at[slot, p], sem.at[0, slot, p]))
            copies.append(pltpu.make_async_copy(cv_hbm.at[page], vbuf.at[slot, p], sem.at[1, slot, p]))
        return copies

    @pl.when(i == 0)
    def _():
        for ahead in range(min(PAGE_SLOTS - 1, n_seq)):
            for c in page_copies(ahead):
                c.start()

    @pl.when(i + PAGE_SLOTS - 1 < n_seq)
    def _():
        for c in page_copies(i + PAGE_SLOTS - 1):
            c.start()

    for c in page_copies(i):
        c.wait()
    slot = i % PAGE_SLOTS
    k_refs = [kbuf.at[slot, p] for p in range(n_pages)]
    v_refs = [vbuf.at[slot, p] for p in range(n_pages)]

    q = q_ref[...]
    r = lax.broadcasted_iota(jnp.int32, q.shape, 0)
    lane = lax.broadcasted_iota(jnp.int32, q.shape, 1)
    diag = (lane // HEAD_DIM) == (r // n_new)
    qbd = jnp.where(diag, q, jnp.zeros_like(q))
    qf = qbd.astype(F32)

    def block_of(refs, j):
        pages = [refs[j * pages_per_blk + i][...].astype(BF16) for i in range(pages_per_blk)]
        return jnp.concatenate(pages, axis=1)

    s_blk = [_dot(qbd, block_of(k_refs, j)) for j in range(n_blk)]
    sb = [jnp.sum(s, axis=-1, keepdims=True) * (1.0 / MOBA_BLOCK) for s in s_blk]

    sel = []
    for j in range(n_blk):
        rank = jnp.zeros(sb[j].shape, jnp.int32)
        for i in range(n_blk):
            if i == j:
                continue
            beats = (sb[i] >= sb[j]) if i < j else (sb[i] > sb[j])
            rank = rank + beats.astype(jnp.int32)
        sel.append(rank < MOBA_TOPK)

    tok = r[:, 0:1] % n_new
    s_past = [jnp.where(sel[j], s_blk[j], MASK_VALUE) for j in range(n_blk)]
    kn = kn_ref[...]
    vn = vn_ref[...]
    s_new = [jnp.where(tok >= i, jnp.sum(qf * kn[i:i + 1, :], axis=-1, keepdims=True), MASK_VALUE)
             for i in range(n_new)]
    m = s_new[0]
    for s in s_new[1:]:
        m = jnp.maximum(m, s)
    for s in s_past:
        m = jnp.maximum(m, jnp.max(s, axis=-1, keepdims=True))

    l = jnp.zeros(m.shape, F32)
    acc = jnp.zeros(q.shape, F32)
    for i in range(n_new):
        p = jnp.exp2(s_new[i] - m)
        l = l + p
        acc = acc + p * vn[i:i + 1, :]
    for j in range(n_blk):
        p = jnp.exp2(s_past[j] - m)
        l = l + jnp.sum(p, axis=-1, keepdims=True)
        acc = acc + _dot_nt(p.astype(BF16), block_of(v_refs, j))
    o = jnp.where(diag, acc / l, 0.0)
    while o.shape[0] > 8:
        half = o.shape[0] // 2
        o = o[:half] + o[half:]
    shift = 4
    while shift >= n_new:
        o = o + pltpu.roll(o, shift, axis=0)
        shift //= 2
    o_ref[...] = o.astype(o_ref.dtype)


def _attn_sample(q, k_new, v_new, cache_k, cache_v, page_table):
    n, t, w = q.shape
    n_pages = page_table.shape[1]
    assert (n_pages * PAGE_SIZE) % MOBA_BLOCK == 0 and MOBA_BLOCK % PAGE_SIZE == 0
    assert t in (1, 2, 4, 8) and n_pages * PAGE_SIZE // MOBA_BLOCK >= MOBA_TOPK
    n_pool = cache_k.shape[0]
    ck = cache_k.transpose(0, 2, 3, 1).reshape(n_pool, w, PAGE_SIZE)
    cv = cache_v.transpose(0, 2, 3, 1).reshape(n_pool, w, PAGE_SIZE)
    q_rep = jnp.tile(q, (1, N_HEADS, 1))

    grid_spec = pltpu.PrefetchScalarGridSpec(
        num_scalar_prefetch=1,
        grid=(n,),
        in_specs=[
            pl.BlockSpec((None, N_HEADS * t, w), lambda i, pt: (i, 0, 0)),
            pl.BlockSpec((None, t, w), lambda i, pt: (i, 0, 0)),
            pl.BlockSpec((None, t, w), lambda i, pt: (i, 0, 0)),
            pl.BlockSpec(memory_space=pl.ANY),
            pl.BlockSpec(memory_space=pl.ANY),
        ],
        out_specs=pl.BlockSpec((None, 8, w), lambda i, pt: (i, 0, 0)),
        scratch_shapes=[
            pltpu.VMEM((PAGE_SLOTS, n_pages, w, PAGE_SIZE), F32),
            pltpu.VMEM((PAGE_SLOTS, n_pages, w, PAGE_SIZE), F32),
            pltpu.SemaphoreType.DMA((2, PAGE_SLOTS, n_pages)),
        ],
    )
    out = pl.pallas_call(
        functools.partial(_attn_sample_kernel, n_pages=n_pages, n_new=t, n_seq=n),
        grid_spec=grid_spec,
        out_shape=jax.ShapeDtypeStruct((n, 8, w), BF16),
        compiler_params=pltpu.CompilerParams(
            dimension_semantics=("arbitrary",), vmem_limit_bytes=VMEM_LIMIT),
        name="attn_sample",
    )(page_table.reshape(-1), q_rep, k_new, v_new, ck, cv)
    return out[:, :t]


def _merge_ffn_kernel(x_ref, ao_ref, co_ref, ga_ref, gc_ref, wpa_ref, wpc_ref, wo_ref,
                      g1_ref, g2_ref, g3_ref, wg_ref, wu_ref, wd_ref, y_ref, a_ref, *, ff_chunk):
    m = ga_ref[...] * _dot(ao_ref[...], wpa_ref[...]) + gc_ref[...] * _dot(co_ref[...], wpc_ref[...])
    h = x_ref[...] + _rms(_dot(m.astype(BF16), wo_ref[...])) * g1_ref[...]
    hn = (_rms(h) * g2_ref[...]).astype(BF16)
    d_ff = wg_ref.shape[1]
    for c in range(0, d_ff, ff_chunk):
        g = _dot(hn, wg_ref[:, c:c + ff_chunk])
        u = _dot(hn, wu_ref[:, c:c + ff_chunk])
        a_ref[:, c:c + ff_chunk] = (g * jax.nn.sigmoid(g) * u).astype(BF16)
    f = _dot(a_ref[...], wd_ref[...])
    y_ref[...] = h + _rms(f) * g3_ref[...]


def _merge_ffn(x, ao, co, ga, gc, wpa, wpc, wo, g1, g2, g3, wg, wu, wd, tm):
    rows, d = x.shape
    cw = ao.shape[1]
    d_ff = wg.shape[1]
    ff_chunk = FF_CHUNK
    assert d_ff % ff_chunk == 0 and rows % tm == 0
    row_blk = lambda width: pl.BlockSpec((tm, width), lambda i: (i, 0))
    return pl.pallas_call(
        functools.partial(_merge_ffn_kernel, ff_chunk=ff_chunk),
        grid=(rows // tm,),
        in_specs=[row_blk(d), row_blk(cw), row_blk(cw), row_blk(d), row_blk(d),
                  _resident(wpa.shape), _resident(wpc.shape), _resident(wo.shape),
                  _resident((1, d)), _resident((1, d)), _resident((1, d)),
                  _resident(wg.shape), _resident(wu.shape), _resident(wd.shape)],
        out_specs=row_blk(d),
        out_shape=jax.ShapeDtypeStruct((rows, d), F32),
        scratch_shapes=[pltpu.VMEM((tm, d_ff), BF16)],
        compiler_params=pltpu.CompilerParams(
            dimension_semantics=("arbitrary",), vmem_limit_bytes=VMEM_LIMIT),
        name="merge_ffn",
    )(x, ao, co, ga, gc, wpa, wpc, wo, g1.reshape(1, d), g2.reshape(1, d), g3.reshape(1, d),
      wg, wu, wd)


def kernel(x_prompt, x_sample, cache_k, cache_v, state_conv, page_table, g_attn_pre, w_in, w_conv, w_proj_attn, w_proj_conv, w_out, g_attn_post, g_ffn_pre, w_gate, w_up, w_down, g_ffn_post):
    depth = w_in.shape[0]
    bsz, s, d = x_prompt.shape
    n, t, _ = x_sample.shape
    cw = ATTN_WIDTH
    assert t >= CONV_WIDTH - 1 and s >= CONV_WIDTH - 1
    xp, xs = x_prompt, x_sample
    outs = [[] for _ in range(6)]
    for l in range(depth):
        w_in_b = w_in[l].astype(BF16)
        tail = (w_proj_attn[l].astype(BF16), w_proj_conv[l].astype(BF16), w_out[l].astype(BF16),
                g_attn_post[l], g_ffn_pre[l], g_ffn_post[l],
                w_gate[l].astype(BF16), w_up[l].astype(BF16), w_down[l].astype(BF16))

        prev0 = jnp.zeros((bsz, CONV_WIDTH - 1, cw), F32)
        q, k, v, co, ga, gc, u_tail = _proj_prompt(xp, g_attn_pre[l], w_in_b, w_conv[l], prev0,
                                                      PROJ_ROWS)
        ao = _attn_prompt(q, k, v)
        flat = lambda a: a.reshape(bsz * s, a.shape[-1])
        xp = _merge_ffn(flat(xp), flat(ao), flat(co), flat(ga), flat(gc), *tail, MERGE_ROWS)
        xp = xp.reshape(bsz, s, d)
        outs[0].append(k.reshape(bsz, s, N_HEADS, HEAD_DIM))
        outs[1].append(v.reshape(bsz, s, N_HEADS, HEAD_DIM))
        outs[2].append(u_tail[:, 8 - (CONV_WIDTH - 1):])

        q, k, v, co, ga, gc, u = _proj_sample(xs, g_attn_pre[l], w_in_b, w_conv[l], state_conv[l])
        ao = _attn_sample(q.reshape(n, t, cw), k.reshape(n, t, cw), v.reshape(n, t, cw),
                          cache_k[l], cache_v[l], page_table)
        xs = _merge_ffn(xs.reshape(n * t, d), ao.reshape(n * t, cw), co, ga, gc, *tail,
                        MERGE_SAMPLE_ROWS)
        xs = xs.reshape(n, t, d)
        outs[3].append(k.reshape(n, t, N_HEADS, HEAD_DIM))
        outs[4].append(v.reshape(n, t, N_HEADS, HEAD_DIM))
        outs[5].append(u.reshape(n, t, cw)[:, t - (CONV_WIDTH - 1):])

    kp, vp, cp, ks, vs, cs = (jnp.stack(o) for o in outs)
    return (xp, xs, kp, vp, cp, ks, vs, cs)
```

```python
import functools

import jax
import jax.numpy as jnp
from jax import lax
from jax.experimental import pallas as pl
from jax.experimental.pallas import tpu as pltpu

N_HEADS = 8
HEAD_DIM = 64
ATTN_WIDTH = N_HEADS * HEAD_DIM
MOBA_BLOCK = 256
MOBA_TOPK = 3
CONV_WIDTH = 3
PAGE_SIZE = 128
RMS_EPS = 1e-6

LANES = 128
HEADS_PER_GROUP = LANES // HEAD_DIM
MASK_VALUE = -(2.0 ** 100)
Q_SCALE = HEAD_DIM ** -0.5 * 1.4426950408889634
VMEM_LIMIT = 52 * 1024 * 1024
PAGE_SLOTS = 3

PROJ_ROWS = 512
PROJ_SAMPLE_ROWS = 128
MERGE_ROWS = 512
MERGE_SAMPLE_ROWS = 256
FF_CHUNK = 256
ATTN_CHUNK = 4

F32 = jnp.float32
BF16 = jnp.bfloat16


def _dot(a, b):
    return jnp.dot(a, b, preferred_element_type=F32)


def _dot_nt(a, b):
    return lax.dot_general(a, b, (((1,), (1,)), ((), ())), preferred_element_type=F32)


def _rms(x):
    return x * lax.rsqrt(jnp.mean(x * x, axis=-1, keepdims=True) + RMS_EPS)


def _resident(shape):
    return pl.BlockSpec(shape, lambda *_: (0,) * len(shape), pipeline_mode=pl.Buffered(1))


def _proj_kernel(*refs, seq_rows, d_model):
    if seq_rows is None:
        (x_ref, g_ref, w_ref, wc_ref, b1_ref, q_ref, k_ref, v_ref, kb_ref, vb_ref, co_ref, ga_ref,
         gc_ref, u_ref, carry_ref) = refs
    else:
        (x_ref, g_ref, w_ref, wc_ref, b1_ref, b2_ref, q_ref, k_ref, v_ref, co_ref, ga_ref, gc_ref,
         u_ref) = refs
    tm = x_ref.shape[0]
    cw = ATTN_WIDTH
    xn = (_rms(x_ref[...]) * g_ref[...]).astype(BF16)

    def proj(lo, width):
        return _dot(xn, w_ref[:, lo:lo + width])

    ga_ref[...] = jax.nn.sigmoid(proj(6 * cw, d_model))
    gc_ref[...] = jax.nn.sigmoid(proj(6 * cw + d_model, d_model))
    cc = proj(4 * cw, cw)
    cx = proj(5 * cw, cw)
    u = cc * cx
    row = lax.broadcasted_iota(jnp.int32, u.shape, 0)
    s1 = pltpu.roll(u, 1, axis=0)
    s2 = pltpu.roll(u, 2, axis=0)
    if seq_rows is None:
        @pl.when(pl.program_id(1) == 0)
        def _():
            carry_ref[...] = b1_ref[0]
        c = carry_ref[...]
        prev1 = jnp.broadcast_to(c[7:8, :], u.shape)
        prev2 = jnp.broadcast_to(c[6:7, :], u.shape)
        s1 = jnp.where(row == 0, prev1, s1)
        s2 = jnp.where(row == 0, prev2, jnp.where(row == 1, prev1, s2))
        carry_ref[...] = u[tm - 8:tm, :]
        u_ref[0] = u[tm - 8:tm, :]
    else:
        t = row % seq_rows
        s1 = jnp.where(t >= 1, s1, b1_ref[...])
        s2 = jnp.where(t >= 2, s2, b2_ref[...])
        u_ref[...] = u
    wc = wc_ref[...]
    y = wc[0:1, :] * s2 + wc[1:2, :] * s1 + wc[2:3, :] * u
    co_ref[...] = (proj(3 * cw, cw) * y).astype(BF16)
    q_ref[...] = (proj(0, cw) * Q_SCALE).astype(BF16)
    k = proj(cw, cw)
    v = proj(2 * cw, cw)
    k_ref[...] = k
    v_ref[...] = v
    if seq_rows is None:
        kb_ref[...] = k.astype(BF16)
        vb_ref[...] = v.astype(BF16)


def _proj_prompt(x, g, w_in, w_conv, prev, tm):
    bsz, s, d = x.shape
    cw = ATTN_WIDTH
    hist = jnp.zeros((bsz, 8, cw), F32).at[:, 8 - (CONV_WIDTH - 1):].set(prev)
    row_blk = lambda width: pl.BlockSpec((None, tm, width), lambda b, i: (b, i, 0))
    outs = pl.pallas_call(
        functools.partial(_proj_kernel, seq_rows=None, d_model=d),
        grid=(bsz, s // tm),
        in_specs=[
            row_blk(d),
            _resident((1, d)),
            _resident(w_in.shape),
            _resident(w_conv.shape),
            pl.BlockSpec((1, 8, cw), lambda b, i: (b, 0, 0)),
        ],
        out_specs=[row_blk(cw), row_blk(cw), row_blk(cw), row_blk(cw), row_blk(cw), row_blk(cw),
                   row_blk(d), row_blk(d), pl.BlockSpec((1, 8, cw), lambda b, i: (b, 0, 0))],
        out_shape=[
            jax.ShapeDtypeStruct((bsz, s, cw), BF16),
            jax.ShapeDtypeStruct((bsz, s, cw), F32),
            jax.ShapeDtypeStruct((bsz, s, cw), F32),
            jax.ShapeDtypeStruct((bsz, s, cw), BF16),
            jax.ShapeDtypeStruct((bsz, s, cw), BF16),
            jax.ShapeDtypeStruct((bsz, s, cw), BF16),
            jax.ShapeDtypeStruct((bsz, s, d), F32),
            jax.ShapeDtypeStruct((bsz, s, d), F32),
            jax.ShapeDtypeStruct((bsz, 8, cw), F32),
        ],
        scratch_shapes=[pltpu.VMEM((8, cw), F32)],
        compiler_params=pltpu.CompilerParams(
            dimension_semantics=("arbitrary", "arbitrary"), vmem_limit_bytes=VMEM_LIMIT),
        name="proj_prompt",
    )(x, g.reshape(1, d), w_in, w_conv, hist)
    return outs


def _proj_sample(x, g, w_in, w_conv, state):
    n, t, d = x.shape
    cw = ATTN_WIDTH
    rows = n * t
    b1 = jnp.zeros((n, t, cw), F32).at[:, 0].set(state[:, 1]).reshape(rows, cw)
    b2 = jnp.zeros((n, t, cw), F32).at[:, 0].set(state[:, 0]).at[:, 1].set(state[:, 1])
    b2 = b2.reshape(rows, cw)
    tm = PROJ_SAMPLE_ROWS
    assert rows % tm == 0 and tm % t == 0
    full = lambda width: pl.BlockSpec((tm, width), lambda i: (i, 0))
    outs = pl.pallas_call(
        functools.partial(_proj_kernel, seq_rows=t, d_model=d),
        grid=(rows // tm,),
        in_specs=[full(d), _resident((1, d)), _resident(w_in.shape), _resident(w_conv.shape),
                  full(cw), full(cw)],
        out_specs=[full(cw), full(cw), full(cw), full(cw), full(d), full(d), full(cw)],
        out_shape=[
            jax.ShapeDtypeStruct((rows, cw), BF16),
            jax.ShapeDtypeStruct((rows, cw), F32),
            jax.ShapeDtypeStruct((rows, cw), F32),
            jax.ShapeDtypeStruct((rows, cw), BF16),
            jax.ShapeDtypeStruct((rows, d), F32),
            jax.ShapeDtypeStruct((rows, d), F32),
            jax.ShapeDtypeStruct((rows, cw), F32),
        ],
        compiler_params=pltpu.CompilerParams(
            dimension_semantics=("arbitrary",), vmem_limit_bytes=VMEM_LIMIT),
        name="proj_sample",
    )(x.reshape(rows, d), g.reshape(1, d), w_in, w_conv, b1, b2)
    return outs


V_ROWS = HEAD_DIM + 16


def _select_blocks(scores, n_valid):
    blk_id = lax.broadcasted_iota(jnp.int32, scores.shape, 0)
    blk_f = blk_id.astype(F32)
    cand = jnp.where(blk_id < n_valid, scores, -jnp.inf)
    sel = jnp.zeros(scores.shape, jnp.bool_)
    for _ in range(MOBA_TOPK):
        mx = jnp.max(cand, axis=0, keepdims=True)
        idx = jnp.min(jnp.where(cand == mx, blk_f, float(scores.shape[0])), axis=0, keepdims=True)
        pick = (blk_f == idx) & (mx > -jnp.inf)
        sel = sel | pick
        cand = jnp.where(pick, -jnp.inf, cand)
    return sel


SEQ_PAIR = 2


def _attn_prompt_kernel(q_ref, qn_ref, k_ref, v_ref, o_ref, kaug_ref, vt_ref, kmean_ref, qaug_ref,
                        s_ref, *, n_blocks, chunk):
    t = pl.program_id(2)
    blk = MOBA_BLOCK
    heads = range(HEADS_PER_GROUP)
    A, B = range(SEQ_PAIR)

    def put_query(a, q_rows_ref, n_valid, slot):
        q_t = q_rows_ref[a].astype(F32).T
        d_row = lax.broadcasted_iota(jnp.int32, q_t.shape, 0)
        kmean = kmean_ref[a].astype(BF16)
        pad_rows = jnp.zeros((LANES - n_blocks, blk), BF16)
        for h in heads:
            in_head = (d_row >= h * HEAD_DIM) & (d_row < (h + 1) * HEAD_DIM)
            qh = jnp.where(in_head, q_t, 0.0).astype(BF16)
            pen = jnp.where(_select_blocks(_dot(kmean, qh), n_valid), 0.0, MASK_VALUE).astype(BF16)
            qaug_ref[a, slot, h] = jnp.concatenate([qh, pen, pad_rows], axis=0)

    @pl.when(t == 0)
    def _init():
        lane_k = lax.broadcasted_iota(jnp.int32, (blk, LANES), 1)
        pad_i = lax.broadcasted_iota(jnp.int32, (V_ROWS - HEAD_DIM, blk), 0)
        ones_rows = (pad_i == 0).astype(BF16)

        def fill(j, c):
            rows = pl.ds(pl.multiple_of(j * blk, blk), blk)
            for a in (A, B):
                kb = k_ref[a, rows, :]
                kaug_ref[a, rows, 0:LANES] = kb
                kaug_ref[a, rows, LANES:2 * LANES] = (lane_k == j).astype(BF16)
                v_t = v_ref[a, rows, :].astype(F32).T
                for h in heads:
                    vt_ref[a, j, h, 0:HEAD_DIM] = v_t[h * HEAD_DIM:(h + 1) * HEAD_DIM, :].astype(BF16)
                    vt_ref[a, j, h, HEAD_DIM:V_ROWS] = ones_rows
                kmean_ref[a, pl.ds(j, 1), :] = (
                    jnp.sum(kb.astype(F32), axis=0, keepdims=True) * (1.0 / blk))
            return c

        lax.fori_loop(0, n_blocks, fill, 0)
        for a in (A, B):
            put_query(a, q_ref, 0, 0)

    cur = t % 2
    q_aug = [[qaug_ref[a, cur, h] for h in heads] for a in (A, B)]
    key_i = lax.broadcasted_iota(jnp.int32, (blk, blk), 0)
    qry_i = lax.broadcasted_iota(jnp.int32, (blk, blk), 1)
    causal = key_i <= qry_i
    own = pl.ds(pl.multiple_of(t * blk, blk), blk)
    last_chunk = n_blocks // chunk - 1

    def put_scores(a, c, slot):
        rows = pl.ds(pl.multiple_of(c * (chunk * blk), chunk * blk), chunk * blk)
        kc = kaug_ref[a, rows, :]
        for h in heads:
            s_ref[a, slot, h] = _dot(kc, q_aug[a][h])

    def consume(a, c, slot, state):
        new = []
        for h in heads:
            m, acc = state[h]
            s = s_ref[a, slot, h]
            m_new = jnp.maximum(m, jnp.max(s, axis=0, keepdims=True))
            p = jnp.exp2(s - m_new).astype(BF16)
            acc = jnp.exp2(m - m_new) * acc
            for i in range(chunk):
                acc = acc + _dot(vt_ref[a, c * chunk + i, h], p[i * blk:(i + 1) * blk, :])
            new.append((m_new, acc))
        return tuple(new)

    own_scores = [[_dot(kaug_ref[a, own, 0:LANES], q_aug[a][h][0:LANES]) for h in heads]
                  for a in (A, B)]
    put_scores(A, 0, 0)
    state = []
    for a in (A, B):
        st = []
        for h in heads:
            s = jnp.where(causal, own_scores[a][h], MASK_VALUE)
            m = jnp.max(s, axis=0, keepdims=True)
            st.append((m, _dot(vt_ref[a, t, h], jnp.exp2(s - m).astype(BF16))))
        state.append(tuple(st))
    for a in (A, B):
        put_query(a, qn_ref, t + 1, 1 - cur)

    n_chunks = (t + chunk - 1) // chunk

    def step_pair(j, state):
        st_a, st_b = state
        put_scores(B, 2 * j, 0)
        st_a = consume(A, 2 * j, 0, st_a)
        put_scores(A, 2 * j + 1, 1)
        st_b = consume(B, 2 * j, 0, st_b)
        put_scores(B, 2 * j + 1, 1)
        st_a = consume(A, 2 * j + 1, 1, st_a)
        put_scores(A, jnp.minimum(2 * j + 2, last_chunk), 0)
        st_b = consume(B, 2 * j + 1, 1, st_b)
        return st_a, st_b

    def odd_tail(state):
        st_a, st_b = state
        put_scores(B, n_chunks - 1, 0)
        st_a = consume(A, n_chunks - 1, 0, st_a)
        return st_a, consume(B, n_chunks - 1, 0, st_b)

    state = lax.fori_loop(0, n_chunks // 2, step_pair, tuple(state))
    fin = lax.cond(n_chunks % 2 == 1, odd_tail, lambda st: st, state)
    for a in (A, B):
        o_t = jnp.concatenate(
            [acc[0:HEAD_DIM] * (1.0 / acc[HEAD_DIM:HEAD_DIM + 1]) for (_, acc) in fin[a]], axis=0)
        o_ref[a] = o_t.T.astype(o_ref.dtype)


def _attn_prompt(q, k, v):
    bsz, s, w = q.shape
    n_blocks = s // MOBA_BLOCK
    chunk = ATTN_CHUNK
    assert n_blocks % (2 * chunk) == 0 and n_blocks % 16 == 0 and n_blocks <= LANES
    assert bsz % SEQ_PAIR == 0
    groups = w // LANES
    q_blk = pl.BlockSpec((SEQ_PAIR, MOBA_BLOCK, LANES), lambda b, g, t: (b, t, g))
    q_next = pl.BlockSpec((SEQ_PAIR, MOBA_BLOCK, LANES),
                          lambda b, g, t: (b, jnp.minimum(t + 1, n_blocks - 1), g))
    kv_blk = pl.BlockSpec((SEQ_PAIR, s, LANES), lambda b, g, t: (b, 0, g))
    return pl.pallas_call(
        functools.partial(_attn_prompt_kernel, n_blocks=n_blocks, chunk=chunk),
        grid=(bsz // SEQ_PAIR, groups, n_blocks),
        in_specs=[q_blk, q_next, kv_blk, kv_blk],
        out_specs=q_blk,
        out_shape=jax.ShapeDtypeStruct((bsz, s, w), BF16),
        scratch_shapes=[
            pltpu.VMEM((SEQ_PAIR, s, 2 * LANES), BF16),
            pltpu.VMEM((SEQ_PAIR, n_blocks, HEADS_PER_GROUP, V_ROWS, MOBA_BLOCK), BF16),
            pltpu.VMEM((SEQ_PAIR, n_blocks, LANES), F32),
            pltpu.VMEM((SEQ_PAIR, 2, HEADS_PER_GROUP, 2 * LANES, MOBA_BLOCK), BF16),
            pltpu.VMEM((SEQ_PAIR, 2, HEADS_PER_GROUP, chunk * MOBA_BLOCK, MOBA_BLOCK), F32),
        ],
        compiler_params=pltpu.CompilerParams(
            dimension_semantics=("arbitrary", "arbitrary", "arbitrary"),
            vmem_limit_bytes=VMEM_LIMIT),
        name="attn_prompt",
    )(q, q, k, v)


def _attn_sample_kernel(pt_ref, q_ref, kn_ref, vn_ref, ck_hbm, cv_hbm, o_ref, kbuf, vbuf, sem, *,
                        n_pages, n_new, n_seq):
    i = pl.program_id(0)
    pages_per_blk = MOBA_BLOCK // PAGE_SIZE
    n_blk = n_pages // pages_per_blk

    def page_copies(seq):
        slot = seq % PAGE_SLOTS
        copies = []
        for p in range(n_pages):
            page = pt_ref[seq * n_pages + p]
            copies.append(pltpu.make_async_copy(ck_hbm.at[page], kbuf.at[slot, p], sem.at[0, slot, p]))
            copies.append(pltpu.make_async_copy(cv_hbm.at[page], vbuf.at[slot, p], sem.at[1, slot, p]))
        return copies

    @pl.when(i == 0)
    def _():
        for ahead in range(min(PAGE_SLOTS - 1, n_seq)):
            for c in page_copies(ahead):
                c.start()

    @pl.when(i + PAGE_SLOTS - 1 < n_seq)
    def _():
        for c in page_copies(i + PAGE_SLOTS - 1):
            c.start()

    for c in page_copies(i):
        c.wait()
    slot = i % PAGE_SLOTS
    k_refs = [kbuf.at[slot, p] for p in range(n_pages)]
    v_refs = [vbuf.at[slot, p] for p in range(n_pages)]

    q = q_ref[...]
    r = lax.broadcasted_iota(jnp.int32, q.shape, 0)
    lane = lax.broadcasted_iota(jnp.int32, q.shape, 1)
    diag = (lane // HEAD_DIM) == (r // n_new)
    qbd = jnp.where(diag, q, jnp.zeros_like(q))
    qf = qbd.astype(F32)

    def block_of(refs, j):
        pages = [refs[j * pages_per_blk + i][...].astype(BF16) for i in range(pages_per_blk)]
        return jnp.concatenate(pages, axis=1)

    s_blk = [_dot(qbd, block_of(k_refs, j)) for j in range(n_blk)]
    sb = [jnp.sum(s, axis=-1, keepdims=True) * (1.0 / MOBA_BLOCK) for s in s_blk]

    sel = []
    for j in range(n_blk):
        rank = jnp.zeros(sb[j].shape, jnp.int32)
        for i in range(n_blk):
            if i == j:
                continue
            beats = (sb[i] >= sb[j]) if i < j else (sb[i] > sb[j])
            rank = rank + beats.astype(jnp.int32)
        sel.append(rank < MOBA_TOPK)

    tok = r[:, 0:1] % n_new
    s_past = [jnp.where(sel[j], s_blk[j], MASK_VALUE) for j in range(n_blk)]
    kn = kn_ref[...]
    vn = vn_ref[...]
    s_new = [jnp.where(tok >= i, jnp.sum(qf * kn[i:i + 1, :], axis=-1, keepdims=True), MASK_VALUE)
             for i in range(n_new)]
    m = s_new[0]
    for s in s_new[1:]:
        m = jnp.maximum(m, s)
    for s in s_past:
        m = jnp.maximum(m, jnp.max(s, axis=-1, keepdims=True))

    l = jnp.zeros(m.shape, F32)
    acc = jnp.zeros(q.shape, F32)
    for i in range(n_new):
        p = jnp.exp2(s_new[i] - m)
        l = l + p
        acc = acc + p * vn[i:i + 1, :]
    for j in range(n_blk):
        p = jnp.exp2(s_past[j] - m)
        l = l + jnp.sum(p, axis=-1, keepdims=True)
        acc = acc + _dot_nt(p.astype(BF16), block_of(v_refs, j))
    o = jnp.where(diag, acc / l, 0.0)
    while o.shape[0] > 8:
        half = o.shape[0] // 2
        o = o[:half] + o[half:]
    shift = 4
    while shift >= n_new:
        o = o + pltpu.roll(o, shift, axis=0)
        shift //= 2
    o_ref[...] = o.astype(o_ref.dtype)


def _attn_sample(q, k_new, v_new, cache_k, cache_v, page_table):
    n, t, w = q.shape
    n_pages = page_table.shape[1]
    assert (n_pages * PAGE_SIZE) % MOBA_BLOCK == 0 and MOBA_BLOCK % PAGE_SIZE == 0
    assert t in (1, 2, 4, 8) and n_pages * PAGE_SIZE // MOBA_BLOCK >= MOBA_TOPK
    n_pool = cache_k.shape[0]
    ck = cache_k.transpose(0, 2, 3, 1).reshape(n_pool, w, PAGE_SIZE)
    cv = cache_v.transpose(0, 2, 3, 1).reshape(n_pool, w, PAGE_SIZE)
    q_rep = jnp.tile(q, (1, N_HEADS, 1))

    grid_spec = pltpu.PrefetchScalarGridSpec(
        num_scalar_prefetch=1,
        grid=(n,),
        in_specs=[
            pl.BlockSpec((None, N_HEADS * t, w), lambda i, pt: (i, 0, 0)),
            pl.BlockSpec((None, t, w), lambda i, pt: (i, 0, 0)),
            pl.BlockSpec((None, t, w), lambda i, pt: (i, 0, 0)),
            pl.BlockSpec(memory_space=pl.ANY),
            pl.BlockSpec(memory_space=pl.ANY),
        ],
        out_specs=pl.BlockSpec((None, 8, w), lambda i, pt: (i, 0, 0)),
        scratch_shapes=[
            pltpu.VMEM((PAGE_SLOTS, n_pages, w, PAGE_SIZE), F32),
            pltpu.VMEM((PAGE_SLOTS, n_pages, w, PAGE_SIZE), F32),
            pltpu.SemaphoreType.DMA((2, PAGE_SLOTS, n_pages)),
        ],
    )
    out = pl.pallas_call(
        functools.partial(_attn_sample_kernel, n_pages=n_pages, n_new=t, n_seq=n),
        grid_spec=grid_spec,
        out_shape=jax.ShapeDtypeStruct((n, 8, w), BF16),
        compiler_params=pltpu.CompilerParams(
            dimension_semantics=("arbitrary",), vmem_limit_bytes=VMEM_LIMIT),
        name="attn_sample",
    )(page_table.reshape(-1), q_rep, k_new, v_new, ck, cv)
    return out[:, :t]


def _merge_ffn_kernel(x_ref, ao_ref, co_ref, ga_ref, gc_ref, wpa_ref, wpc_ref, wo_ref,
                      g1_ref, g2_ref, g3_ref, wg_ref, wu_ref, wd_ref, y_ref, a_ref, *, ff_chunk):
    m = ga_ref[...] * _dot(ao_ref[...], wpa_ref[...]) + gc_ref[...] * _dot(co_ref[...], wpc_ref[...])
    h = x_ref[...] + _rms(_dot(m.astype(BF16), wo_ref[...])) * g1_ref[...]
    hn = (_rms(h) * g2_ref[...]).astype(BF16)
    d_ff = wg_ref.shape[1]
    for c in range(0, d_ff, ff_chunk):
        g = _dot(hn, wg_ref[:, c:c + ff_chunk])
        u = _dot(hn, wu_ref[:, c:c + ff_chunk])
        a_ref[:, c:c + ff_chunk] = (g * jax.nn.sigmoid(g) * u).astype(BF16)
    f = _dot(a_ref[...], wd_ref[...])
    y_ref[...] = h + _rms(f) * g3_ref[...]


def _merge_ffn(x, ao, co, ga, gc, wpa, wpc, wo, g1, g2, g3, wg, wu, wd, tm):
    rows, d = x.shape
    cw = ao.shape[1]
    d_ff = wg.shape[1]
    ff_chunk = FF_CHUNK
    assert d_ff % ff_chunk == 0 and rows % tm == 0
    row_blk = lambda width: pl.BlockSpec((tm, width), lambda i: (i, 0))
    return pl.pallas_call(
        functools.partial(_merge_ffn_kernel, ff_chunk=ff_chunk),
        grid=(rows // tm,),
        in_specs=[row_blk(d), row_blk(cw), row_blk(cw), row_blk(d), row_blk(d),
                  _resident(wpa.shape), _resident(wpc.shape), _resident(wo.shape),
                  _resident((1, d)), _resident((1, d)), _resident((1, d)),
                  _resident(wg.shape), _resident(wu.shape), _resident(wd.shape)],
        out_specs=row_blk(d),
        out_shape=jax.ShapeDtypeStruct((rows, d), F32),
        scratch_shapes=[pltpu.VMEM((tm, d_ff), BF16)],
        compiler_params=pltpu.CompilerParams(
            dimension_semantics=("arbitrary",), vmem_limit_bytes=VMEM_LIMIT),
        name="merge_ffn",
    )(x, ao, co, ga, gc, wpa, wpc, wo, g1.reshape(1, d), g2.reshape(1, d), g3.reshape(1, d),
      wg, wu, wd)


def kernel(x_prompt, x_sample, cache_k, cache_v, state_conv, page_table, g_attn_pre, w_in, w_conv, w_proj_attn, w_proj_conv, w_out, g_attn_post, g_ffn_pre, w_gate, w_up, w_down, g_ffn_post):
    depth = w_in.shape[0]
    bsz, s, d = x_prompt.shape
    n, t, _ = x_sample.shape
    cw = ATTN_WIDTH
    assert t >= CONV_WIDTH - 1 and s >= CONV_WIDTH - 1
    xp, xs = x_prompt, x_sample
    outs = [[] for _ in range(6)]
    for l in range(depth):
        w_in_b = w_in[l].astype(BF16)
        tail = (w_proj_attn[l].astype(BF16), w_proj_conv[l].astype(BF16), w_out[l].astype(BF16),
                g_attn_post[l], g_ffn_pre[l], g_ffn_post[l],
                w_gate[l].astype(BF16), w_up[l].astype(BF16), w_down[l].astype(BF16))

        prev0 = jnp.zeros((bsz, CONV_WIDTH - 1, cw), F32)
        q, k, v, kb, vb, co, ga, gc, u_tail = _proj_prompt(xp, g_attn_pre[l], w_in_b, w_conv[l],
                                                              prev0, PROJ_ROWS)
        ao = _attn_prompt(q, kb, vb)
        flat = lambda a: a.reshape(bsz * s, a.shape[-1])
        xp = _merge_ffn(flat(xp), flat(ao), flat(co), flat(ga), flat(gc), *tail, MERGE_ROWS)
        xp = xp.reshape(bsz, s, d)
        outs[0].append(k.reshape(bsz, s, N_HEADS, HEAD_DIM))
        outs[1].append(v.reshape(bsz, s, N_HEADS, HEAD_DIM))
        outs[2].append(u_tail[:, 8 - (CONV_WIDTH - 1):])

        q, k, v, co, ga, gc, u = _proj_sample(xs, g_attn_pre[l], w_in_b, w_conv[l], state_conv[l])
        ao = _attn_sample(q.reshape(n, t, cw), k.reshape(n, t, cw), v.reshape(n, t, cw),
                          cache_k[l], cache_v[l], page_table)
        xs = _merge_ffn(xs.reshape(n * t, d), ao.reshape(n * t, cw), co, ga, gc, *tail,
                        MERGE_SAMPLE_ROWS)
        xs = xs.reshape(n, t, d)
        outs[3].append(k.reshape(n, t, N_HEADS, HEAD_DIM))
        outs[4].append(v.reshape(n, t, N_HEADS, HEAD_DIM))
        outs[5].append(u.reshape(n, t, cw)[:, t - (CONV_WIDTH - 1):])

    kp, vp, cp, ks, vs, cs = (jnp.stack(o) for o in outs)
    return (xp, xs, kp, vp, cp, ks, vs, cs)
```

```python
import functools

import jax
import jax.numpy as jnp
from jax import lax
from jax.experimental import pallas as pl
from jax.experimental.pallas import tpu as pltpu

N_HEADS = 8
HEAD_DIM = 64
ATTN_WIDTH = N_HEADS * HEAD_DIM
MOBA_BLOCK = 256
MOBA_TOPK = 3
CONV_WIDTH = 3
PAGE_SIZE = 128
RMS_EPS = 1e-6

LANES = 128
HEADS_PER_GROUP = LANES // HEAD_DIM
MASK_VALUE = -(2.0 ** 100)
Q_SCALE = HEAD_DIM ** -0.5 * 1.4426950408889634
VMEM_LIMIT = 52 * 1024 * 1024
PAGE_SLOTS = 3

PROJ_ROWS = 512
PROJ_SAMPLE_ROWS = 128
MERGE_ROWS = 512
MERGE_SAMPLE_ROWS = 256
FF_CHUNK = 256
ATTN_CHUNK = 4

F32 = jnp.float32
BF16 = jnp.bfloat16


def _dot(a, b):
    return jnp.dot(a, b, preferred_element_type=F32)


def _dot_nt(a, b):
    return lax.dot_general(a, b, (((1,), (1,)), ((), ())), preferred_element_type=F32)


def _rms(x):
    return x * lax.rsqrt(jnp.mean(x * x, axis=-1, keepdims=True) + RMS_EPS)


def _resident(shape):
    return pl.BlockSpec(shape, lambda *_: (0,) * len(shape), pipeline_mode=pl.Buffered(1))


def _proj_kernel(*refs, seq_rows, d_model):
    if seq_rows is None:
        (x_ref, g_ref, w_ref, wc_ref, b1_ref, q_ref, k_ref, v_ref, kb_ref, vb_ref, co_ref, ga_ref,
         gc_ref, u_ref, carry_ref) = refs
    else:
        (x_ref, g_ref, w_ref, wc_ref, b1_ref, b2_ref, q_ref, k_ref, v_ref, co_ref, ga_ref, gc_ref,
         u_ref) = refs
    tm = x_ref.shape[0]
    cw = ATTN_WIDTH
    xn = (_rms(x_ref[...]) * g_ref[...]).astype(BF16)

    def proj(lo, width):
        return _dot(xn, w_ref[:, lo:lo + width])

    ga_ref[...] = jax.nn.sigmoid(proj(6 * cw, d_model))
    gc_ref[...] = jax.nn.sigmoid(proj(6 * cw + d_model, d_model))
    cc = proj(4 * cw, cw)
    cx = proj(5 * cw, cw)
    u = cc * cx
    row = lax.broadcasted_iota(jnp.int32, u.shape, 0)
    s1 = pltpu.roll(u, 1, axis=0)
    s2 = pltpu.roll(u, 2, axis=0)
    if seq_rows is None:
        @pl.when(pl.program_id(1) == 0)
        def _():
            carry_ref[...] = b1_ref[0]
        c = carry_ref[...]
        prev1 = jnp.broadcast_to(c[7:8, :], u.shape)
        prev2 = jnp.broadcast_to(c[6:7, :], u.shape)
        s1 = jnp.where(row == 0, prev1, s1)
        s2 = jnp.where(row == 0, prev2, jnp.where(row == 1, prev1, s2))
        carry_ref[...] = u[tm - 8:tm, :]
        u_ref[0] = u[tm - 8:tm, :]
    else:
        t = row % seq_rows
        s1 = jnp.where(t >= 1, s1, b1_ref[...])
        s2 = jnp.where(t >= 2, s2, b2_ref[...])
        u_ref[...] = u
    wc = wc_ref[...]
    y = wc[0:1, :] * s2 + wc[1:2, :] * s1 + wc[2:3, :] * u
    co_ref[...] = (proj(3 * cw, cw) * y).astype(BF16)
    q_ref[...] = (proj(0, cw) * Q_SCALE).astype(BF16)
    k = proj(cw, cw)
    v = proj(2 * cw, cw)
    k_ref[...] = k
    v_ref[...] = v
    if seq_rows is None:
        kb_ref[...] = k.astype(BF16)
        vb_ref[...] = v.astype(BF16)


def _proj_prompt(x, g, w_in, w_conv, prev, tm):
    bsz, s, d = x.shape
    cw = ATTN_WIDTH
    hist = jnp.zeros((bsz, 8, cw), F32).at[:, 8 - (CONV_WIDTH - 1):].set(prev)
    row_blk = lambda width: pl.BlockSpec((None, tm, width), lambda b, i: (b, i, 0))
    outs = pl.pallas_call(
        functools.partial(_proj_kernel, seq_rows=None, d_model=d),
        grid=(bsz, s // tm),
        in_specs=[
            row_blk(d),
            _resident((1, d)),
            _resident(w_in.shape),
            _resident(w_conv.shape),
            pl.BlockSpec((1, 8, cw), lambda b, i: (b, 0, 0)),
        ],
        out_specs=[row_blk(cw), row_blk(cw), row_blk(cw), row_blk(cw), row_blk(cw), row_blk(cw),
                   row_blk(d), row_blk(d), pl.BlockSpec((1, 8, cw), lambda b, i: (b, 0, 0))],
        out_shape=[
            jax.ShapeDtypeStruct((bsz, s, cw), BF16),
            jax.ShapeDtypeStruct((bsz, s, cw), F32),
            jax.ShapeDtypeStruct((bsz, s, cw), F32),
            jax.ShapeDtypeStruct((bsz, s, cw), BF16),
            jax.ShapeDtypeStruct((bsz, s, cw), BF16),
            jax.ShapeDtypeStruct((bsz, s, cw), BF16),
            jax.ShapeDtypeStruct((bsz, s, d), F32),
            jax.ShapeDtypeStruct((bsz, s, d), F32),
            jax.ShapeDtypeStruct((bsz, 8, cw), F32),
        ],
        scratch_shapes=[pltpu.VMEM((8, cw), F32)],
        compiler_params=pltpu.CompilerParams(
            dimension_semantics=("arbitrary", "arbitrary"), vmem_limit_bytes=VMEM_LIMIT),
        name="proj_prompt",
    )(x, g.reshape(1, d), w_in, w_conv, hist)
    return outs


def _proj_sample(x, g, w_in, w_conv, state):
    n, t, d = x.shape
    cw = ATTN_WIDTH
    rows = n * t
    b1 = jnp.zeros((n, t, cw), F32).at[:, 0].set(state[:, 1]).reshape(rows, cw)
    b2 = jnp.zeros((n, t, cw), F32).at[:, 0].set(state[:, 0]).at[:, 1].set(state[:, 1])
    b2 = b2.reshape(rows, cw)
    tm = PROJ_SAMPLE_ROWS
    assert rows % tm == 0 and tm % t == 0
    full = lambda width: pl.BlockSpec((tm, width), lambda i: (i, 0))
    outs = pl.pallas_call(
        functools.partial(_proj_kernel, seq_rows=t, d_model=d),
        grid=(rows // tm,),
        in_specs=[full(d), _resident((1, d)), _resident(w_in.shape), _resident(w_conv.shape),
                  full(cw), full(cw)],
        out_specs=[full(cw), full(cw), full(cw), full(cw), full(d), full(d), full(cw)],
        out_shape=[
            jax.ShapeDtypeStruct((rows, cw), BF16),
            jax.ShapeDtypeStruct((rows, cw), F32),
            jax.ShapeDtypeStruct((rows, cw), F32),
            jax.ShapeDtypeStruct((rows, cw), BF16),
            jax.ShapeDtypeStruct((rows, d), F32),
            jax.ShapeDtypeStruct((rows, d), F32),
            jax.ShapeDtypeStruct((rows, cw), F32),
        ],
        compiler_params=pltpu.CompilerParams(
            dimension_semantics=("arbitrary",), vmem_limit_bytes=VMEM_LIMIT),
        name="proj_sample",
    )(x.reshape(rows, d), g.reshape(1, d), w_in, w_conv, b1, b2)
    return outs


V_ROWS = HEAD_DIM + 16


def _select_blocks(scores, n_valid):
    blk_id = lax.broadcasted_iota(jnp.int32, scores.shape, 0)
    blk_f = blk_id.astype(F32)
    cand = jnp.where(blk_id < n_valid, scores, -jnp.inf)
    sel = jnp.zeros(scores.shape, jnp.bool_)
    for _ in range(MOBA_TOPK):
        mx = jnp.max(cand, axis=0, keepdims=True)
        idx = jnp.min(jnp.where(cand == mx, blk_f, float(scores.shape[0])), axis=0, keepdims=True)
        pick = (blk_f == idx) & (mx > -jnp.inf)
        sel = sel | pick
        cand = jnp.where(pick, -jnp.inf, cand)
    return sel


SEQ_PAIR = 2


def _attn_prompt_kernel(q_ref, qn_ref, k_ref, v_ref, o_ref, kaug_ref, vt_ref, kmean_ref, qaug_ref,
                        s_ref, *, n_blocks, chunk):
    t = pl.program_id(2)
    blk = MOBA_BLOCK
    heads = range(HEADS_PER_GROUP)
    A, B = range(SEQ_PAIR)

    def put_query(a, q_rows_ref, n_valid, slot):
        q_t = q_rows_ref[a].astype(F32).T
        d_row = lax.broadcasted_iota(jnp.int32, q_t.shape, 0)
        kmean = kmean_ref[a].astype(BF16)
        pad_rows = jnp.zeros((LANES - n_blocks, blk), BF16)
        for h in heads:
            in_head = (d_row >= h * HEAD_DIM) & (d_row < (h + 1) * HEAD_DIM)
            qh = jnp.where(in_head, q_t, 0.0).astype(BF16)
            pen = jnp.where(_select_blocks(_dot(kmean, qh), n_valid), 0.0, MASK_VALUE).astype(BF16)
            qaug_ref[a, slot, h] = jnp.concatenate([qh, pen, pad_rows], axis=0)

    @pl.when(t == 0)
    def _init():
        lane_k = lax.broadcasted_iota(jnp.int32, (blk, LANES), 1)
        pad_i = lax.broadcasted_iota(jnp.int32, (V_ROWS - HEAD_DIM, blk), 0)
        ones_rows = (pad_i == 0).astype(BF16)

        def fill(j, c):
            rows = pl.ds(pl.multiple_of(j * blk, blk), blk)
            for a in (A, B):
                kb = k_ref[a, rows, :]
                kaug_ref[a, rows, 0:LANES] = kb
                kaug_ref[a, rows, LANES:2 * LANES] = (lane_k == j).astype(BF16)
                v_t = v_ref[a, rows, :].astype(F32).T
                for h in heads:
                    vt_ref[a, j, h, 0:HEAD_DIM] = v_t[h * HEAD_DIM:(h + 1) * HEAD_DIM, :].astype(BF16)
                    vt_ref[a, j, h, HEAD_DIM:V_ROWS] = ones_rows
                kmean_ref[a, pl.ds(j, 1), :] = (
                    jnp.sum(kb.astype(F32), axis=0, keepdims=True) * (1.0 / blk))
            return c

        lax.fori_loop(0, n_blocks, fill, 0)
        for a in (A, B):
            put_query(a, q_ref, 0, 0)

    cur = t % 2
    q_aug = [[qaug_ref[a, cur, h] for h in heads] for a in (A, B)]
    key_i = lax.broadcasted_iota(jnp.int32, (blk, blk), 0)
    qry_i = lax.broadcasted_iota(jnp.int32, (blk, blk), 1)
    causal = key_i <= qry_i
    own = pl.ds(pl.multiple_of(t * blk, blk), blk)
    last_chunk = n_blocks // chunk - 1

    def put_scores(a, c, slot):
        rows = pl.ds(pl.multiple_of(c * (chunk * blk), chunk * blk), chunk * blk)
        kc = kaug_ref[a, rows, :]
        for h in heads:
            s_ref[a, slot, h] = _dot(kc, q_aug[a][h])

    def consume(a, c, slot, state):
        new = []
        for h in heads:
            m, acc = state[h]
            s = s_ref[a, slot, h]
            m_new = jnp.maximum(m, jnp.max(s, axis=0, keepdims=True))
            p = jnp.exp2(s - m_new).astype(BF16)
            acc = jnp.exp2(m - m_new) * acc
            for i in range(chunk):
                acc = acc + _dot(vt_ref[a, c * chunk + i, h], p[i * blk:(i + 1) * blk, :])
            new.append((m_new, acc))
        return tuple(new)

    own_scores = [[_dot(kaug_ref[a, own, 0:LANES], q_aug[a][h][0:LANES]) for h in heads]
                  for a in (A, B)]
    put_scores(A, 0, 0)
    state = []
    for a in (A, B):
        st = []
        for h in heads:
            s = jnp.where(causal, own_scores[a][h], MASK_VALUE)
            m = jnp.max(s, axis=0, keepdims=True)
            st.append((m, _dot(vt_ref[a, t, h], jnp.exp2(s - m).astype(BF16))))
        state.append(tuple(st))
    for a in (A, B):
        put_query(a, qn_ref, t + 1, 1 - cur)

    n_chunks = (t + chunk - 1) // chunk

    def step_pair(j, state):
        st_a, st_b = state
        put_scores(B, 2 * j, 0)
        st_a = consume(A, 2 * j, 0, st_a)
        put_scores(A, 2 * j + 1, 1)
        st_b = consume(B, 2 * j, 0, st_b)
        put_scores(B, 2 * j + 1, 1)
        st_a = consume(A, 2 * j + 1, 1, st_a)
        put_scores(A, jnp.minimum(2 * j + 2, last_chunk), 0)
        st_b = consume(B, 2 * j + 1, 1, st_b)
        return st_a, st_b

    def odd_tail(state):
        st_a, st_b = state
        put_scores(B, n_chunks - 1, 0)
        st_a = consume(A, n_chunks - 1, 0, st_a)
        return st_a, consume(B, n_chunks - 1, 0, st_b)

    state = lax.fori_loop(0, n_chunks // 2, step_pair, tuple(state))
    fin = lax.cond(n_chunks % 2 == 1, odd_tail, lambda st: st, state)
    for a in (A, B):
        o_t = jnp.concatenate(
            [acc[0:HEAD_DIM] * (1.0 / acc[HEAD_DIM:HEAD_DIM + 1]) for (_, acc) in fin[a]], axis=0)
        o_ref[a] = o_t.T.astype(o_ref.dtype)


def _attn_prompt(q, k, v):
    bsz, s, w = q.shape
    n_blocks = s // MOBA_BLOCK
    chunk = ATTN_CHUNK
    assert n_blocks % (2 * chunk) == 0 and n_blocks % 16 == 0 and n_blocks <= LANES
    assert bsz % SEQ_PAIR == 0
    groups = w // LANES
    q_blk = pl.BlockSpec((SEQ_PAIR, MOBA_BLOCK, LANES), lambda b, g, t: (b, t, g))
    q_next = pl.BlockSpec((SEQ_PAIR, MOBA_BLOCK, LANES),
                          lambda b, g, t: (b, jnp.minimum(t + 1, n_blocks - 1), g))
    kv_blk = pl.BlockSpec((SEQ_PAIR, s, LANES), lambda b, g, t: (b, 0, g))
    return pl.pallas_call(
        functools.partial(_attn_prompt_kernel, n_blocks=n_blocks, chunk=chunk),
        grid=(bsz // SEQ_PAIR, groups, n_blocks),
        in_specs=[q_blk, q_next, kv_blk, kv_blk],
        out_specs=q_blk,
        out_shape=jax.ShapeDtypeStruct((bsz, s, w), BF16),
        scratch_shapes=[
            pltpu.VMEM((SEQ_PAIR, s, 2 * LANES), BF16),
            pltpu.VMEM((SEQ_PAIR, n_blocks, HEADS_PER_GROUP, V_ROWS, MOBA_BLOCK), BF16),
            pltpu.VMEM((SEQ_PAIR, n_blocks, LANES), F32),
            pltpu.VMEM((SEQ_PAIR, 2, HEADS_PER_GROUP, 2 * LANES, MOBA_BLOCK), BF16),
            pltpu.VMEM((SEQ_PAIR, 2, HEADS_PER_GROUP, chunk * MOBA_BLOCK, MOBA_BLOCK), F32),
        ],
        compiler_params=pltpu.CompilerParams(
            dimension_semantics=("arbitrary", "arbitrary", "arbitrary"),
            vmem_limit_bytes=VMEM_LIMIT),
        name="attn_prompt",
    )(q, q, k, v)


def _attn_sample_kernel(pt_ref, q_ref, kn_ref, vn_ref, ck_hbm, cv_hbm, o_ref, kbuf, vbuf, sem, *,
                        n_pages, n_new, n_seq):
    this_seq = pl.program_id(0)
    pages_per_blk = MOBA_BLOCK // PAGE_SIZE
    n_blk = n_pages // pages_per_blk

    def page_copies(seq):
        slot = seq % PAGE_SLOTS
        copies = []
        for p in range(n_pages):
            page = pt_ref[seq * n_pages + p]
            copies.append(pltpu.make_async_copy(ck_hbm.at[page], kbuf.at[slot, p], sem.at[0, slot, p]))
            copies.append(pltpu.make_async_copy(cv_hbm.at[page], vbuf.at[slot, p], sem.at[1, slot, p]))
        return copies

    @pl.when(this_seq == 0)
    def _():
        for ahead in range(min(PAGE_SLOTS - 1, n_seq)):
            for c in page_copies(ahead):
                c.start()

    @pl.when(this_seq + PAGE_SLOTS - 1 < n_seq)
    def _():
        for c in page_copies(this_seq + PAGE_SLOTS - 1):
            c.start()

    for c in page_copies(this_seq):
        c.wait()
    slot = this_seq % PAGE_SLOTS
    k_refs = [kbuf.at[slot, p] for p in range(n_pages)]
    v_refs = [vbuf.at[slot, p] for p in range(n_pages)]

    q = q_ref[...]
    r = lax.broadcasted_iota(jnp.int32, q.shape, 0)
    lane = lax.broadcasted_iota(jnp.int32, q.shape, 1)
    diag = (lane // HEAD_DIM) == (r // n_new)
    qbd = jnp.where(diag, q, jnp.zeros_like(q))
    qf = qbd.astype(F32)

    def block_of(refs, j):
        pages = [refs[j * pages_per_blk + i][...].astype(BF16) for i in range(pages_per_blk)]
        return jnp.concatenate(pages, axis=1)

    s_blk = [_dot(qbd, block_of(k_refs, j)) for j in range(n_blk)]
    sb = [jnp.sum(s, axis=-1, keepdims=True) * (1.0 / MOBA_BLOCK) for s in s_blk]

    sel = []
    for j in range(n_blk):
        rank = jnp.zeros(sb[j].shape, jnp.int32)
        for i in range(n_blk):
            if i == j:
                continue
            beats = (sb[i] >= sb[j]) if i < j else (sb[i] > sb[j])
            rank = rank + beats.astype(jnp.int32)
        sel.append(rank < MOBA_TOPK)

    tok = r[:, 0:1] % n_new
    s_past = [jnp.where(sel[j], s_blk[j], MASK_VALUE) for j in range(n_blk)]
    def own_rows(ref):
        rows8 = ref[...]
        out = rows8[0:n_new]
        for c in range(1, 8 // n_new):
            out = jnp.where(this_seq % (8 // n_new) == c, rows8[c * n_new:(c + 1) * n_new], out)
        return out

    kn = own_rows(kn_ref)
    vn = own_rows(vn_ref)
    s_new = [jnp.where(tok >= i, jnp.sum(qf * kn[i:i + 1, :], axis=-1, keepdims=True), MASK_VALUE)
             for i in range(n_new)]
    m = s_new[0]
    for s in s_new[1:]:
        m = jnp.maximum(m, s)
    for s in s_past:
        m = jnp.maximum(m, jnp.max(s, axis=-1, keepdims=True))

    l = jnp.zeros(m.shape, F32)
    acc = jnp.zeros(q.shape, F32)
    for i in range(n_new):
        p = jnp.exp2(s_new[i] - m)
        l = l + p
        acc = acc + p * vn[i:i + 1, :]
    for j in range(n_blk):
        p = jnp.exp2(s_past[j] - m)
        l = l + jnp.sum(p, axis=-1, keepdims=True)
        acc = acc + _dot_nt(p.astype(BF16), block_of(v_refs, j))
    o = jnp.where(diag, acc / l, 0.0)
    while o.shape[0] > 8:
        half = o.shape[0] // 2
        o = o[:half] + o[half:]
    shift = 4
    while shift >= n_new:
        o = o + pltpu.roll(o, shift, axis=0)
        shift //= 2
    o_ref[...] = o.astype(o_ref.dtype)


def _attn_sample(q, k_new, v_new, cache_k, cache_v, page_table):
    n, t, w = q.shape
    n_pages = page_table.shape[1]
    assert (n_pages * PAGE_SIZE) % MOBA_BLOCK == 0 and MOBA_BLOCK % PAGE_SIZE == 0
    assert t in (1, 2, 4, 8) and n_pages * PAGE_SIZE // MOBA_BLOCK >= MOBA_TOPK
    assert k_new.shape == (n * t, w) and (n * t) % 8 == 0
    n_pool = cache_k.shape[0]
    ck = cache_k.transpose(0, 2, 3, 1).reshape(n_pool, w, PAGE_SIZE)
    cv = cache_v.transpose(0, 2, 3, 1).reshape(n_pool, w, PAGE_SIZE)
    q_rep = jnp.tile(q, (1, N_HEADS, 1))

    grid_spec = pltpu.PrefetchScalarGridSpec(
        num_scalar_prefetch=1,
        grid=(n,),
        in_specs=[
            pl.BlockSpec((None, N_HEADS * t, w), lambda i, pt: (i, 0, 0)),
            pl.BlockSpec((8, w), lambda i, pt: (i // (8 // t), 0)),
            pl.BlockSpec((8, w), lambda i, pt: (i // (8 // t), 0)),
            pl.BlockSpec(memory_space=pl.ANY),
            pl.BlockSpec(memory_space=pl.ANY),
        ],
        out_specs=pl.BlockSpec((None, 8, w), lambda i, pt: (i, 0, 0)),
        scratch_shapes=[
            pltpu.VMEM((PAGE_SLOTS, n_pages, w, PAGE_SIZE), F32),
            pltpu.VMEM((PAGE_SLOTS, n_pages, w, PAGE_SIZE), F32),
            pltpu.SemaphoreType.DMA((2, PAGE_SLOTS, n_pages)),
        ],
    )
    out = pl.pallas_call(
        functools.partial(_attn_sample_kernel, n_pages=n_pages, n_new=t, n_seq=n),
        grid_spec=grid_spec,
        out_shape=jax.ShapeDtypeStruct((n, 8, w), BF16),
        compiler_params=pltpu.CompilerParams(
            dimension_semantics=("arbitrary",), vmem_limit_bytes=VMEM_LIMIT),
        name="attn_sample",
    )(page_table.reshape(-1), q_rep, k_new, v_new, ck, cv)
    return out[:, :t]


def _merge_ffn_kernel(x_ref, ao_ref, co_ref, ga_ref, gc_ref, wpa_ref, wpc_ref, wo_ref,
                      g1_ref, g2_ref, g3_ref, wg_ref, wu_ref, wd_ref, y_ref, a_ref, *, ff_chunk):
    m = ga_ref[...] * _dot(ao_ref[...], wpa_ref[...]) + gc_ref[...] * _dot(co_ref[...], wpc_ref[...])
    h = x_ref[...] + _rms(_dot(m.astype(BF16), wo_ref[...])) * g1_ref[...]
    hn = (_rms(h) * g2_ref[...]).astype(BF16)
    d_ff = wg_ref.shape[1]
    for c in range(0, d_ff, ff_chunk):
        g = _dot(hn, wg_ref[:, c:c + ff_chunk])
        u = _dot(hn, wu_ref[:, c:c + ff_chunk])
        a_ref[:, c:c + ff_chunk] = (g * jax.nn.sigmoid(g) * u).astype(BF16)
    f = _dot(a_ref[...], wd_ref[...])
    y_ref[...] = h + _rms(f) * g3_ref[...]


def _merge_ffn(x, ao, co, ga, gc, wpa, wpc, wo, g1, g2, g3, wg, wu, wd, tm):
    rows, d = x.shape
    cw = ao.shape[1]
    d_ff = wg.shape[1]
    ff_chunk = FF_CHUNK
    assert d_ff % ff_chunk == 0 and rows % tm == 0
    row_blk = lambda width: pl.BlockSpec((tm, width), lambda i: (i, 0))
    return pl.pallas_call(
        functools.partial(_merge_ffn_kernel, ff_chunk=ff_chunk),
        grid=(rows // tm,),
        in_specs=[row_blk(d), row_blk(cw), row_blk(cw), row_blk(d), row_blk(d),
                  _resident(wpa.shape), _resident(wpc.shape), _resident(wo.shape),
                  _resident((1, d)), _resident((1, d)), _resident((1, d)),
                  _resident(wg.shape), _resident(wu.shape), _resident(wd.shape)],
        out_specs=row_blk(d),
        out_shape=jax.ShapeDtypeStruct((rows, d), F32),
        scratch_shapes=[pltpu.VMEM((tm, d_ff), BF16)],
        compiler_params=pltpu.CompilerParams(
            dimension_semantics=("arbitrary",), vmem_limit_bytes=VMEM_LIMIT),
        name="merge_ffn",
    )(x, ao, co, ga, gc, wpa, wpc, wo, g1.reshape(1, d), g2.reshape(1, d), g3.reshape(1, d),
      wg, wu, wd)


def kernel(x_prompt, x_sample, cache_k, cache_v, state_conv, page_table, g_attn_pre, w_in, w_conv, w_proj_attn, w_proj_conv, w_out, g_attn_post, g_ffn_pre, w_gate, w_up, w_down, g_ffn_post):
    depth = w_in.shape[0]
    bsz, s, d = x_prompt.shape
    n, t, _ = x_sample.shape
    cw = ATTN_WIDTH
    assert t >= CONV_WIDTH - 1 and s >= CONV_WIDTH - 1
    xp, xs = x_prompt, x_sample
    outs = [[] for _ in range(6)]
    for l in range(depth):
        w_in_b = w_in[l].astype(BF16)
        tail = (w_proj_attn[l].astype(BF16), w_proj_conv[l].astype(BF16), w_out[l].astype(BF16),
                g_attn_post[l], g_ffn_pre[l], g_ffn_post[l],
                w_gate[l].astype(BF16), w_up[l].astype(BF16), w_down[l].astype(BF16))

        prev0 = jnp.zeros((bsz, CONV_WIDTH - 1, cw), F32)
        q, k, v, kb, vb, co, ga, gc, u_tail = _proj_prompt(xp, g_attn_pre[l], w_in_b, w_conv[l],
                                                              prev0, PROJ_ROWS)
        ao = _attn_prompt(q, kb, vb)
        flat = lambda a: a.reshape(bsz * s, a.shape[-1])
        xp = _merge_ffn(flat(xp), flat(ao), flat(co), flat(ga), flat(gc), *tail, MERGE_ROWS)
        xp = xp.reshape(bsz, s, d)
        outs[0].append(k.reshape(bsz, s, N_HEADS, HEAD_DIM))
        outs[1].append(v.reshape(bsz, s, N_HEADS, HEAD_DIM))
        outs[2].append(u_tail[:, 8 - (CONV_WIDTH - 1):])

        q, k, v, co, ga, gc, u = _proj_sample(xs, g_attn_pre[l], w_in_b, w_conv[l], state_conv[l])
        ao = _attn_sample(q.reshape(n, t, cw), k, v, cache_k[l], cache_v[l], page_table)
        xs = _merge_ffn(xs.reshape(n * t, d), ao.reshape(n * t, cw), co, ga, gc, *tail,
                        MERGE_SAMPLE_ROWS)
        xs = xs.reshape(n, t, d)
        outs[3].append(k.reshape(n, t, N_HEADS, HEAD_DIM))
        outs[4].append(v.reshape(n, t, N_HEADS, HEAD_DIM))
        outs[5].append(u.reshape(n, t, cw)[:, t - (CONV_WIDTH - 1):])

    kp, vp, cp, ks, vs, cs = (jnp.stack(o) for o in outs)
    return (xp, xs, kp, vp, cp, ks, vs, cs)
```

```python
import functools

import jax
import jax.numpy as jnp
from jax import lax
from jax.experimental import pallas as pl
from jax.experimental.pallas import tpu as pltpu

N_HEADS = 8
HEAD_DIM = 64
ATTN_WIDTH = N_HEADS * HEAD_DIM
MOBA_BLOCK = 256
MOBA_TOPK = 3
CONV_WIDTH = 3
PAGE_SIZE = 128
RMS_EPS = 1e-6

LANES = 128
HEADS_PER_GROUP = LANES // HEAD_DIM
MASK_VALUE = -(2.0 ** 100)
Q_SCALE = HEAD_DIM ** -0.5 * 1.4426950408889634
VMEM_LIMIT = 52 * 1024 * 1024
PAGE_SLOTS = 3

PROJ_ROWS = 512
PROJ_SAMPLE_ROWS = 128
MERGE_ROWS = 512
MERGE_SAMPLE_ROWS = 256
FF_CHUNK = 256
ATTN_CHUNK = 4

F32 = jnp.float32
BF16 = jnp.bfloat16


def _dot(a, b):
    return jnp.dot(a, b, preferred_element_type=F32)


def _dot_nt(a, b):
    return lax.dot_general(a, b, (((1,), (1,)), ((), ())), preferred_element_type=F32)


def _rms(x):
    return x * lax.rsqrt(jnp.mean(x * x, axis=-1, keepdims=True) + RMS_EPS)


def _resident(shape):
    return pl.BlockSpec(shape, lambda *_: (0,) * len(shape), pipeline_mode=pl.Buffered(1))


def _proj_kernel(*refs, seq_rows, d_model):
    if seq_rows is None:
        (x_ref, g_ref, w_ref, wc_ref, b1_ref, q_ref, k_ref, v_ref, kb_ref, vb_ref, co_ref, ga_ref,
         gc_ref, u_ref, carry_ref) = refs
    else:
        (x_ref, g_ref, w_ref, wc_ref, b1_ref, b2_ref, q_ref, k_ref, v_ref, co_ref, ga_ref, gc_ref,
         u_ref) = refs
    tm = x_ref.shape[0]
    cw = ATTN_WIDTH
    xn = (_rms(x_ref[...]) * g_ref[...]).astype(BF16)

    def proj(lo, width):
        return _dot(xn, w_ref[:, lo:lo + width])

    ga_ref[...] = jax.nn.sigmoid(proj(6 * cw, d_model))
    gc_ref[...] = jax.nn.sigmoid(proj(6 * cw + d_model, d_model))
    cc = proj(4 * cw, cw)
    cx = proj(5 * cw, cw)
    u = cc * cx
    row = lax.broadcasted_iota(jnp.int32, u.shape, 0)
    s1 = pltpu.roll(u, 1, axis=0)
    s2 = pltpu.roll(u, 2, axis=0)
    if seq_rows is None:
        @pl.when(pl.program_id(1) == 0)
        def _():
            carry_ref[...] = b1_ref[0]
        c = carry_ref[...]
        prev1 = jnp.broadcast_to(c[7:8, :], u.shape)
        prev2 = jnp.broadcast_to(c[6:7, :], u.shape)
        s1 = jnp.where(row == 0, prev1, s1)
        s2 = jnp.where(row == 0, prev2, jnp.where(row == 1, prev1, s2))
        carry_ref[...] = u[tm - 8:tm, :]
        u_ref[0] = u[tm - 8:tm, :]
    else:
        t = row % seq_rows
        s1 = jnp.where(t >= 1, s1, b1_ref[...])
        s2 = jnp.where(t >= 2, s2, b2_ref[...])
        u_ref[...] = u
    wc = wc_ref[...]
    y = wc[0:1, :] * s2 + wc[1:2, :] * s1 + wc[2:3, :] * u
    co_ref[...] = (proj(3 * cw, cw) * y).astype(BF16)
    q_ref[...] = (proj(0, cw) * Q_SCALE).astype(BF16)
    k = proj(cw, cw)
    v = proj(2 * cw, cw)
    k_ref[...] = k
    v_ref[...] = v
    if seq_rows is None:
        kb_ref[...] = k.astype(BF16)
        vb_ref[...] = v.astype(BF16)


def _proj_prompt(x, g, w_in, w_conv, prev, tm):
    bsz, s, d = x.shape
    cw = ATTN_WIDTH
    hist = jnp.zeros((bsz, 8, cw), F32).at[:, 8 - (CONV_WIDTH - 1):].set(prev)
    row_blk = lambda width: pl.BlockSpec((None, tm, width), lambda b, i: (b, i, 0))
    outs = pl.pallas_call(
        functools.partial(_proj_kernel, seq_rows=None, d_model=d),
        grid=(bsz, s // tm),
        in_specs=[
            row_blk(d),
            _resident((1, d)),
            _resident(w_in.shape),
            _resident(w_conv.shape),
            pl.BlockSpec((1, 8, cw), lambda b, i: (b, 0, 0)),
        ],
        out_specs=[row_blk(cw), row_blk(cw), row_blk(cw), row_blk(cw), row_blk(cw), row_blk(cw),
                   row_blk(d), row_blk(d), pl.BlockSpec((1, 8, cw), lambda b, i: (b, 0, 0))],
        out_shape=[
            jax.ShapeDtypeStruct((bsz, s, cw), BF16),
            jax.ShapeDtypeStruct((bsz, s, cw), F32),
            jax.ShapeDtypeStruct((bsz, s, cw), F32),
            jax.ShapeDtypeStruct((bsz, s, cw), BF16),
            jax.ShapeDtypeStruct((bsz, s, cw), BF16),
            jax.ShapeDtypeStruct((bsz, s, cw), BF16),
            jax.ShapeDtypeStruct((bsz, s, d), F32),
            jax.ShapeDtypeStruct((bsz, s, d), F32),
            jax.ShapeDtypeStruct((bsz, 8, cw), F32),
        ],
        scratch_shapes=[pltpu.VMEM((8, cw), F32)],
        compiler_params=pltpu.CompilerParams(
            dimension_semantics=("arbitrary", "arbitrary"), vmem_limit_bytes=VMEM_LIMIT),
        name="proj_prompt",
    )(x, g.reshape(1, d), w_in, w_conv, hist)
    return outs


def _proj_sample(x, g, w_in, w_conv, state):
    n, t, d = x.shape
    cw = ATTN_WIDTH
    rows = n * t
    b1 = jnp.zeros((n, t, cw), F32).at[:, 0].set(state[:, 1]).reshape(rows, cw)
    b2 = jnp.zeros((n, t, cw), F32).at[:, 0].set(state[:, 0]).at[:, 1].set(state[:, 1])
    b2 = b2.reshape(rows, cw)
    tm = PROJ_SAMPLE_ROWS
    assert rows % tm == 0 and tm % t == 0
    full = lambda width: pl.BlockSpec((tm, width), lambda i: (i, 0))
    outs = pl.pallas_call(
        functools.partial(_proj_kernel, seq_rows=t, d_model=d),
        grid=(rows // tm,),
        in_specs=[full(d), _resident((1, d)), _resident(w_in.shape), _resident(w_conv.shape),
                  full(cw), full(cw)],
        out_specs=[full(cw), full(cw), full(cw), full(cw), full(d), full(d), full(cw)],
        out_shape=[
            jax.ShapeDtypeStruct((rows, cw), BF16),
            jax.ShapeDtypeStruct((rows, cw), F32),
            jax.ShapeDtypeStruct((rows, cw), F32),
            jax.ShapeDtypeStruct((rows, cw), BF16),
            jax.ShapeDtypeStruct((rows, d), F32),
            jax.ShapeDtypeStruct((rows, d), F32),
            jax.ShapeDtypeStruct((rows, cw), F32),
        ],
        compiler_params=pltpu.CompilerParams(
            dimension_semantics=("arbitrary",), vmem_limit_bytes=VMEM_LIMIT),
        name="proj_sample",
    )(x.reshape(rows, d), g.reshape(1, d), w_in, w_conv, b1, b2)
    return outs


V_ROWS = HEAD_DIM + 16


def _select_blocks(scores, n_valid):
    blk_id = lax.broadcasted_iota(jnp.int32, scores.shape, 0)
    blk_f = blk_id.astype(F32)
    cand = jnp.where(blk_id < n_valid, scores, -jnp.inf)
    sel = jnp.zeros(scores.shape, jnp.bool_)
    for _ in range(MOBA_TOPK):
        mx = jnp.max(cand, axis=0, keepdims=True)
        idx = jnp.min(jnp.where(cand == mx, blk_f, float(scores.shape[0])), axis=0, keepdims=True)
        pick = (blk_f == idx) & (mx > -jnp.inf)
        sel = sel | pick
        cand = jnp.where(pick, -jnp.inf, cand)
    return sel


SEQ_PAIR = 2


def _attn_prompt_kernel(q_ref, qn_ref, k_ref, v_ref, o_ref, kaug_ref, vt_ref, kmean_ref, qaug_ref,
                        s_ref, *, n_blocks, chunk):
    t = pl.program_id(2)
    blk = MOBA_BLOCK
    heads = range(HEADS_PER_GROUP)
    A, B = range(SEQ_PAIR)

    def put_query(a, q_rows_ref, n_valid, slot):
        q_t = q_rows_ref[a].astype(F32).T
        d_row = lax.broadcasted_iota(jnp.int32, q_t.shape, 0)
        kmean = kmean_ref[a].astype(BF16)
        pad_rows = jnp.zeros((LANES - n_blocks, blk), BF16)
        for h in heads:
            in_head = (d_row >= h * HEAD_DIM) & (d_row < (h + 1) * HEAD_DIM)
            qh = jnp.where(in_head, q_t, 0.0).astype(BF16)
            pen = jnp.where(_select_blocks(_dot(kmean, qh), n_valid), 0.0, MASK_VALUE).astype(BF16)
            qaug_ref[a, slot, h] = jnp.concatenate([qh, pen, pad_rows], axis=0)

    @pl.when(t == 0)
    def _init():
        lane_k = lax.broadcasted_iota(jnp.int32, (blk, LANES), 1)
        pad_i = lax.broadcasted_iota(jnp.int32, (V_ROWS - HEAD_DIM, blk), 0)
        ones_rows = (pad_i == 0).astype(BF16)

        def fill(j, c):
            rows = pl.ds(pl.multiple_of(j * blk, blk), blk)
            for a in (A, B):
                kb = k_ref[a, rows, :]
                kaug_ref[a, rows, 0:LANES] = kb
                kaug_ref[a, rows, LANES:2 * LANES] = (lane_k == j).astype(BF16)
                v_t = v_ref[a, rows, :].astype(F32).T
                for h in heads:
                    vt_ref[a, j, h, 0:HEAD_DIM] = v_t[h * HEAD_DIM:(h + 1) * HEAD_DIM, :].astype(BF16)
                    vt_ref[a, j, h, HEAD_DIM:V_ROWS] = ones_rows
                kmean_ref[a, pl.ds(j, 1), :] = (
                    jnp.sum(kb.astype(F32), axis=0, keepdims=True) * (1.0 / blk))
            return c

        lax.fori_loop(0, n_blocks, fill, 0)
        for a in (A, B):
            put_query(a, q_ref, 0, 0)

    cur = t % 2
    q_aug = [[qaug_ref[a, cur, h] for h in heads] for a in (A, B)]
    key_i = lax.broadcasted_iota(jnp.int32, (blk, blk), 0)
    qry_i = lax.broadcasted_iota(jnp.int32, (blk, blk), 1)
    causal = key_i <= qry_i
    own = pl.ds(pl.multiple_of(t * blk, blk), blk)
    last_chunk = n_blocks // chunk - 1

    def put_scores(a, c, slot):
        rows = pl.ds(pl.multiple_of(c * (chunk * blk), chunk * blk), chunk * blk)
        kc = kaug_ref[a, rows, :]
        for h in heads:
            s_ref[a, slot, h] = _dot(kc, q_aug[a][h])

    def consume(a, c, slot, state):
        new = []
        for h in heads:
            m, acc = state[h]
            s = s_ref[a, slot, h]
            m_new = jnp.maximum(m, jnp.max(s, axis=0, keepdims=True))
            p = jnp.exp2(s - m_new).astype(BF16)
            acc = jnp.exp2(m - m_new) * acc
            for i in range(chunk):
                acc = acc + _dot(vt_ref[a, c * chunk + i, h], p[i * blk:(i + 1) * blk, :])
            new.append((m_new, acc))
        return tuple(new)

    own_scores = [[_dot(kaug_ref[a, own, 0:LANES], q_aug[a][h][0:LANES]) for h in heads]
                  for a in (A, B)]
    put_scores(A, 0, 0)
    for a in (A, B):
        put_query(a, qn_ref, t + 1, 1 - cur)
    state = []
    for a in (A, B):
        st = []
        for h in heads:
            s = jnp.where(causal, own_scores[a][h], MASK_VALUE)
            m = jnp.max(s, axis=0, keepdims=True)
            st.append((m, _dot(vt_ref[a, t, h], jnp.exp2(s - m).astype(BF16))))
        state.append(tuple(st))

    n_chunks = (t + chunk - 1) // chunk

    def step_pair(j, state):
        st_a, st_b = state
        put_scores(B, 2 * j, 0)
        st_a = consume(A, 2 * j, 0, st_a)
        put_scores(A, 2 * j + 1, 1)
        st_b = consume(B, 2 * j, 0, st_b)
        put_scores(B, 2 * j + 1, 1)
        st_a = consume(A, 2 * j + 1, 1, st_a)
        put_scores(A, jnp.minimum(2 * j + 2, last_chunk), 0)
        st_b = consume(B, 2 * j + 1, 1, st_b)
        return st_a, st_b

    def odd_tail(state):
        st_a, st_b = state
        put_scores(B, n_chunks - 1, 0)
        st_a = consume(A, n_chunks - 1, 0, st_a)
        return st_a, consume(B, n_chunks - 1, 0, st_b)

    state = lax.fori_loop(0, n_chunks // 2, step_pair, tuple(state))
    fin = lax.cond(n_chunks % 2 == 1, odd_tail, lambda st: st, state)
    for a in (A, B):
        o_t = jnp.concatenate(
            [acc[0:HEAD_DIM] * (1.0 / acc[HEAD_DIM:HEAD_DIM + 1]) for (_, acc) in fin[a]], axis=0)
        o_ref[a] = o_t.T.astype(o_ref.dtype)


def _attn_prompt(q, k, v):
    bsz, s, w = q.shape
    n_blocks = s // MOBA_BLOCK
    chunk = ATTN_CHUNK
    assert n_blocks % (2 * chunk) == 0 and n_blocks % 16 == 0 and n_blocks <= LANES
    assert bsz % SEQ_PAIR == 0
    groups = w // LANES
    q_blk = pl.BlockSpec((SEQ_PAIR, MOBA_BLOCK, LANES), lambda b, g, t: (b, t, g))
    q_next = pl.BlockSpec((SEQ_PAIR, MOBA_BLOCK, LANES),
                          lambda b, g, t: (b, jnp.minimum(t + 1, n_blocks - 1), g))
    kv_blk = pl.BlockSpec((SEQ_PAIR, s, LANES), lambda b, g, t: (b, 0, g))
    return pl.pallas_call(
        functools.partial(_attn_prompt_kernel, n_blocks=n_blocks, chunk=chunk),
        grid=(bsz // SEQ_PAIR, groups, n_blocks),
        in_specs=[q_blk, q_next, kv_blk, kv_blk],
        out_specs=q_blk,
        out_shape=jax.ShapeDtypeStruct((bsz, s, w), BF16),
        scratch_shapes=[
            pltpu.VMEM((SEQ_PAIR, s, 2 * LANES), BF16),
            pltpu.VMEM((SEQ_PAIR, n_blocks, HEADS_PER_GROUP, V_ROWS, MOBA_BLOCK), BF16),
            pltpu.VMEM((SEQ_PAIR, n_blocks, LANES), F32),
            pltpu.VMEM((SEQ_PAIR, 2, HEADS_PER_GROUP, 2 * LANES, MOBA_BLOCK), BF16),
            pltpu.VMEM((SEQ_PAIR, 2, HEADS_PER_GROUP, chunk * MOBA_BLOCK, MOBA_BLOCK), F32),
        ],
        compiler_params=pltpu.CompilerParams(
            dimension_semantics=("arbitrary", "arbitrary", "arbitrary"),
            vmem_limit_bytes=VMEM_LIMIT),
        name="attn_prompt",
    )(q, q, k, v)


def _attn_sample_kernel(pt_ref, q_ref, kn_ref, vn_ref, ck_hbm, cv_hbm, o_ref, kbuf, vbuf, sem, *,
                        n_pages, n_new, n_seq):
    this_seq = pl.program_id(0)
    pages_per_blk = MOBA_BLOCK // PAGE_SIZE
    n_blk = n_pages // pages_per_blk

    def page_copies(seq):
        slot = seq % PAGE_SLOTS
        copies = []
        for p in range(n_pages):
            page = pt_ref[seq * n_pages + p]
            copies.append(pltpu.make_async_copy(ck_hbm.at[page], kbuf.at[slot, p], sem.at[0, slot, p]))
            copies.append(pltpu.make_async_copy(cv_hbm.at[page], vbuf.at[slot, p], sem.at[1, slot, p]))
        return copies

    @pl.when(this_seq == 0)
    def _():
        for ahead in range(min(PAGE_SLOTS - 1, n_seq)):
            for c in page_copies(ahead):
                c.start()

    @pl.when(this_seq + PAGE_SLOTS - 1 < n_seq)
    def _():
        for c in page_copies(this_seq + PAGE_SLOTS - 1):
            c.start()

    for c in page_copies(this_seq):
        c.wait()
    slot = this_seq % PAGE_SLOTS
    k_refs = [kbuf.at[slot, p] for p in range(n_pages)]
    v_refs = [vbuf.at[slot, p] for p in range(n_pages)]

    q = q_ref[...]
    r = lax.broadcasted_iota(jnp.int32, q.shape, 0)
    lane = lax.broadcasted_iota(jnp.int32, q.shape, 1)
    diag = (lane // HEAD_DIM) == (r // n_new)
    qbd = jnp.where(diag, q, jnp.zeros_like(q))
    qf = qbd.astype(F32)

    def block_of(refs, j):
        pages = [refs[j * pages_per_blk + i][...].astype(BF16) for i in range(pages_per_blk)]
        return jnp.concatenate(pages, axis=1)

    s_blk = [_dot(qbd, block_of(k_refs, j)) for j in range(n_blk)]
    sb = [jnp.sum(s, axis=-1, keepdims=True) * (1.0 / MOBA_BLOCK) for s in s_blk]

    sel = []
    for j in range(n_blk):
        rank = jnp.zeros(sb[j].shape, jnp.int32)
        for i in range(n_blk):
            if i == j:
                continue
            beats = (sb[i] >= sb[j]) if i < j else (sb[i] > sb[j])
            rank = rank + beats.astype(jnp.int32)
        sel.append(rank < MOBA_TOPK)

    tok = r[:, 0:1] % n_new
    s_past = [jnp.where(sel[j], s_blk[j], MASK_VALUE) for j in range(n_blk)]
    def own_rows(ref):
        rows8 = ref[...]
        out = rows8[0:n_new]
        for c in range(1, 8 // n_new):
            out = jnp.where(this_seq % (8 // n_new) == c, rows8[c * n_new:(c + 1) * n_new], out)
        return out

    kn = own_rows(kn_ref)
    vn = own_rows(vn_ref)
    s_new = [jnp.where(tok >= i, jnp.sum(qf * kn[i:i + 1, :], axis=-1, keepdims=True), MASK_VALUE)
             for i in range(n_new)]
    m = s_new[0]
    for s in s_new[1:]:
        m = jnp.maximum(m, s)
    for s in s_past:
        m = jnp.maximum(m, jnp.max(s, axis=-1, keepdims=True))

    l = jnp.zeros(m.shape, F32)
    acc = jnp.zeros(q.shape, F32)
    for i in range(n_new):
        p = jnp.exp2(s_new[i] - m)
        l = l + p
        acc = acc + p * vn[i:i + 1, :]
    for j in range(n_blk):
        p = jnp.exp2(s_past[j] - m)
        l = l + jnp.sum(p, axis=-1, keepdims=True)
        acc = acc + _dot_nt(p.astype(BF16), block_of(v_refs, j))
    o = jnp.where(diag, acc / l, 0.0)
    while o.shape[0] > 8:
        half = o.shape[0] // 2
        o = o[:half] + o[half:]
    shift = 4
    while shift >= n_new:
        o = o + pltpu.roll(o, shift, axis=0)
        shift //= 2
    o_ref[...] = o.astype(o_ref.dtype)


def _attn_sample(q, k_new, v_new, cache_k, cache_v, page_table):
    n, t, w = q.shape
    n_pages = page_table.shape[1]
    assert (n_pages * PAGE_SIZE) % MOBA_BLOCK == 0 and MOBA_BLOCK % PAGE_SIZE == 0
    assert t in (1, 2, 4, 8) and n_pages * PAGE_SIZE // MOBA_BLOCK >= MOBA_TOPK
    assert k_new.shape == (n * t, w) and (n * t) % 8 == 0
    n_pool = cache_k.shape[0]
    ck = cache_k.transpose(0, 2, 3, 1).reshape(n_pool, w, PAGE_SIZE)
    cv = cache_v.transpose(0, 2, 3, 1).reshape(n_pool, w, PAGE_SIZE)
    q_rep = jnp.tile(q, (1, N_HEADS, 1))

    grid_spec = pltpu.PrefetchScalarGridSpec(
        num_scalar_prefetch=1,
        grid=(n,),
        in_specs=[
            pl.BlockSpec((None, N_HEADS * t, w), lambda i, pt: (i, 0, 0)),
            pl.BlockSpec((8, w), lambda i, pt: (i // (8 // t), 0)),
            pl.BlockSpec((8, w), lambda i, pt: (i // (8 // t), 0)),
            pl.BlockSpec(memory_space=pl.ANY),
            pl.BlockSpec(memory_space=pl.ANY),
        ],
        out_specs=pl.BlockSpec((None, 8, w), lambda i, pt: (i, 0, 0)),
        scratch_shapes=[
            pltpu.VMEM((PAGE_SLOTS, n_pages, w, PAGE_SIZE), F32),
            pltpu.VMEM((PAGE_SLOTS, n_pages, w, PAGE_SIZE), F32),
            pltpu.SemaphoreType.DMA((2, PAGE_SLOTS, n_pages)),
        ],
    )
    out = pl.pallas_call(
        functools.partial(_attn_sample_kernel, n_pages=n_pages, n_new=t, n_seq=n),
        grid_spec=grid_spec,
        out_shape=jax.ShapeDtypeStruct((n, 8, w), BF16),
        compiler_params=pltpu.CompilerParams(
            dimension_semantics=("arbitrary",), vmem_limit_bytes=VMEM_LIMIT),
        name="attn_sample",
    )(page_table.reshape(-1), q_rep, k_new, v_new, ck, cv)
    return out[:, :t]


def _merge_ffn_kernel(x_ref, ao_ref, co_ref, ga_ref, gc_ref, wpa_ref, wpc_ref, wo_ref,
                      g1_ref, g2_ref, g3_ref, wg_ref, wu_ref, wd_ref, y_ref, a_ref, *, ff_chunk):
    m = ga_ref[...] * _dot(ao_ref[...], wpa_ref[...]) + gc_ref[...] * _dot(co_ref[...], wpc_ref[...])
    h = x_ref[...] + _rms(_dot(m.astype(BF16), wo_ref[...])) * g1_ref[...]
    hn = (_rms(h) * g2_ref[...]).astype(BF16)
    d_ff = wg_ref.shape[1]
    for c in range(0, d_ff, ff_chunk):
        g = _dot(hn, wg_ref[:, c:c + ff_chunk])
        u = _dot(hn, wu_ref[:, c:c + ff_chunk])
        a_ref[:, c:c + ff_chunk] = (g * jax.nn.sigmoid(g) * u).astype(BF16)
    f = _dot(a_ref[...], wd_ref[...])
    y_ref[...] = h + _rms(f) * g3_ref[...]


def _merge_ffn(x, ao, co, ga, gc, wpa, wpc, wo, g1, g2, g3, wg, wu, wd, tm):
    rows, d = x.shape
    cw = ao.shape[1]
    d_ff = wg.shape[1]
    ff_chunk = FF_CHUNK
    assert d_ff % ff_chunk == 0 and rows % tm == 0
    row_blk = lambda width: pl.BlockSpec((tm, width), lambda i: (i, 0))
    return pl.pallas_call(
        functools.partial(_merge_ffn_kernel, ff_chunk=ff_chunk),
        grid=(rows // tm,),
        in_specs=[row_blk(d), row_blk(cw), row_blk(cw), row_blk(d), row_blk(d),
                  _resident(wpa.shape), _resident(wpc.shape), _resident(wo.shape),
                  _resident((1, d)), _resident((1, d)), _resident((1, d)),
                  _resident(wg.shape), _resident(wu.shape), _resident(wd.shape)],
        out_specs=row_blk(d),
        out_shape=jax.ShapeDtypeStruct((rows, d), F32),
        scratch_shapes=[pltpu.VMEM((tm, d_ff), BF16)],
        compiler_params=pltpu.CompilerParams(
            dimension_semantics=("arbitrary",), vmem_limit_bytes=VMEM_LIMIT),
        name="merge_ffn",
    )(x, ao, co, ga, gc, wpa, wpc, wo, g1.reshape(1, d), g2.reshape(1, d), g3.reshape(1, d),
      wg, wu, wd)


def kernel(x_prompt, x_sample, cache_k, cache_v, state_conv, page_table, g_attn_pre, w_in, w_conv, w_proj_attn, w_proj_conv, w_out, g_attn_post, g_ffn_pre, w_gate, w_up, w_down, g_ffn_post):
    depth = w_in.shape[0]
    bsz, s, d = x_prompt.shape
    n, t, _ = x_sample.shape
    cw = ATTN_WIDTH
    assert t >= CONV_WIDTH - 1 and s >= CONV_WIDTH - 1
    xp, xs = x_prompt, x_sample
    outs = [[] for _ in range(6)]
    for l in range(depth):
        w_in_b = w_in[l].astype(BF16)
        tail = (w_proj_attn[l].astype(BF16), w_proj_conv[l].astype(BF16), w_out[l].astype(BF16),
                g_attn_post[l], g_ffn_pre[l], g_ffn_post[l],
                w_gate[l].astype(BF16), w_up[l].astype(BF16), w_down[l].astype(BF16))

        prev0 = jnp.zeros((bsz, CONV_WIDTH - 1, cw), F32)
        q, k, v, kb, vb, co, ga, gc, u_tail = _proj_prompt(xp, g_attn_pre[l], w_in_b, w_conv[l],
                                                              prev0, PROJ_ROWS)
        ao = _attn_prompt(q, kb, vb)
        flat = lambda a: a.reshape(bsz * s, a.shape[-1])
        xp = _merge_ffn(flat(xp), flat(ao), flat(co), flat(ga), flat(gc), *tail, MERGE_ROWS)
        xp = xp.reshape(bsz, s, d)
        outs[0].append(k.reshape(bsz, s, N_HEADS, HEAD_DIM))
        outs[1].append(v.reshape(bsz, s, N_HEADS, HEAD_DIM))
        outs[2].append(u_tail[:, 8 - (CONV_WIDTH - 1):])

        q, k, v, co, ga, gc, u = _proj_sample(xs, g_attn_pre[l], w_in_b, w_conv[l], state_conv[l])
        ao = _attn_sample(q.reshape(n, t, cw), k, v, cache_k[l], cache_v[l], page_table)
        xs = _merge_ffn(xs.reshape(n * t, d), ao.reshape(n * t, cw), co, ga, gc, *tail,
                        MERGE_SAMPLE_ROWS)
        xs = xs.reshape(n, t, d)
        outs[3].append(k.reshape(n, t, N_HEADS, HEAD_DIM))
        outs[4].append(v.reshape(n, t, N_HEADS, HEAD_DIM))
        outs[5].append(u.reshape(n, t, cw)[:, t - (CONV_WIDTH - 1):])

    kp, vp, cp, ks, vs, cs = (jnp.stack(o) for o in outs)
    return (xp, xs, kp, vp, cp, ks, vs, cs)
```

```python
import functools

import jax
import jax.numpy as jnp
from jax import lax
from jax.experimental import pallas as pl
from jax.experimental.pallas import tpu as pltpu

N_HEADS = 8
HEAD_DIM = 64
ATTN_WIDTH = N_HEADS * HEAD_DIM
MOBA_BLOCK = 256
MOBA_TOPK = 3
CONV_WIDTH = 3
PAGE_SIZE = 128
RMS_EPS = 1e-6

LANES = 128
HEADS_PER_GROUP = LANES // HEAD_DIM
MASK_VALUE = -(2.0 ** 100)
Q_SCALE = HEAD_DIM ** -0.5 * 1.4426950408889634
VMEM_LIMIT = 52 * 1024 * 1024
PAGE_SLOTS = 4

PROJ_ROWS = 512
PROJ_SAMPLE_ROWS = 128
MERGE_ROWS = 512
MERGE_SAMPLE_ROWS = 256
FF_CHUNK = 256
ATTN_CHUNK = 4

F32 = jnp.float32
BF16 = jnp.bfloat16


def _dot(a, b):
    return jnp.dot(a, b, preferred_element_type=F32)


def _dot_nt(a, b):
    return lax.dot_general(a, b, (((1,), (1,)), ((), ())), preferred_element_type=F32)


def _rms(x):
    return x * lax.rsqrt(jnp.mean(x * x, axis=-1, keepdims=True) + RMS_EPS)


def _resident(shape):
    return pl.BlockSpec(shape, lambda *_: (0,) * len(shape), pipeline_mode=pl.Buffered(1))


def _proj_kernel(*refs, seq_rows, d_model):
    if seq_rows is None:
        (x_ref, g_ref, w_ref, wc_ref, b1_ref, q_ref, k_ref, v_ref, kb_ref, vb_ref, co_ref, ga_ref,
         gc_ref, u_ref, carry_ref) = refs
    else:
        (x_ref, g_ref, w_ref, wc_ref, b1_ref, b2_ref, q_ref, k_ref, v_ref, co_ref, ga_ref, gc_ref,
         u_ref) = refs
    tm = x_ref.shape[0]
    cw = ATTN_WIDTH
    xn = (_rms(x_ref[...]) * g_ref[...]).astype(BF16)

    def proj(lo, width):
        return _dot(xn, w_ref[:, lo:lo + width])

    ga_ref[...] = jax.nn.sigmoid(proj(6 * cw, d_model))
    gc_ref[...] = jax.nn.sigmoid(proj(6 * cw + d_model, d_model))
    cc = proj(4 * cw, cw)
    cx = proj(5 * cw, cw)
    u = cc * cx
    row = lax.broadcasted_iota(jnp.int32, u.shape, 0)
    s1 = pltpu.roll(u, 1, axis=0)
    s2 = pltpu.roll(u, 2, axis=0)
    if seq_rows is None:
        @pl.when(pl.program_id(1) == 0)
        def _():
            carry_ref[...] = b1_ref[0]
        c = carry_ref[...]
        prev1 = jnp.broadcast_to(c[7:8, :], u.shape)
        prev2 = jnp.broadcast_to(c[6:7, :], u.shape)
        s1 = jnp.where(row == 0, prev1, s1)
        s2 = jnp.where(row == 0, prev2, jnp.where(row == 1, prev1, s2))
        carry_ref[...] = u[tm - 8:tm, :]
        u_ref[0] = u[tm - 8:tm, :]
    else:
        t = row % seq_rows
        s1 = jnp.where(t >= 1, s1, b1_ref[...])
        s2 = jnp.where(t >= 2, s2, b2_ref[...])
        u_ref[...] = u
    wc = wc_ref[...]
    y = wc[0:1, :] * s2 + wc[1:2, :] * s1 + wc[2:3, :] * u
    co_ref[...] = (proj(3 * cw, cw) * y).astype(BF16)
    q_ref[...] = (proj(0, cw) * Q_SCALE).astype(BF16)
    k = proj(cw, cw)
    v = proj(2 * cw, cw)
    k_ref[...] = k
    v_ref[...] = v
    if seq_rows is None:
        kb_ref[...] = k.astype(BF16)
        vb_ref[...] = v.astype(BF16)


def _proj_prompt(x, g, w_in, w_conv, prev, tm):
    bsz, s, d = x.shape
    cw = ATTN_WIDTH
    hist = jnp.zeros((bsz, 8, cw), F32).at[:, 8 - (CONV_WIDTH - 1):].set(prev)
    row_blk = lambda width: pl.BlockSpec((None, tm, width), lambda b, i: (b, i, 0))
    outs = pl.pallas_call(
        functools.partial(_proj_kernel, seq_rows=None, d_model=d),
        grid=(bsz, s // tm),
        in_specs=[
            row_blk(d),
            _resident((1, d)),
            _resident(w_in.shape),
            _resident(w_conv.shape),
            pl.BlockSpec((1, 8, cw), lambda b, i: (b, 0, 0)),
        ],
        out_specs=[row_blk(cw), row_blk(cw), row_blk(cw), row_blk(cw), row_blk(cw), row_blk(cw),
                   row_blk(d), row_blk(d), pl.BlockSpec((1, 8, cw), lambda b, i: (b, 0, 0))],
        out_shape=[
            jax.ShapeDtypeStruct((bsz, s, cw), BF16),
            jax.ShapeDtypeStruct((bsz, s, cw), F32),
            jax.ShapeDtypeStruct((bsz, s, cw), F32),
            jax.ShapeDtypeStruct((bsz, s, cw), BF16),
            jax.ShapeDtypeStruct((bsz, s, cw), BF16),
            jax.ShapeDtypeStruct((bsz, s, cw), BF16),
            jax.ShapeDtypeStruct((bsz, s, d), F32),
            jax.ShapeDtypeStruct((bsz, s, d), F32),
            jax.ShapeDtypeStruct((bsz, 8, cw), F32),
        ],
        scratch_shapes=[pltpu.VMEM((8, cw), F32)],
        compiler_params=pltpu.CompilerParams(
            dimension_semantics=("arbitrary", "arbitrary"), vmem_limit_bytes=VMEM_LIMIT),
        name="proj_prompt",
    )(x, g.reshape(1, d), w_in, w_conv, hist)
    return outs


def _proj_sample(x, g, w_in, w_conv, state):
    n, t, d = x.shape
    cw = ATTN_WIDTH
    rows = n * t
    b1 = jnp.zeros((n, t, cw), F32).at[:, 0].set(state[:, 1]).reshape(rows, cw)
    b2 = jnp.zeros((n, t, cw), F32).at[:, 0].set(state[:, 0]).at[:, 1].set(state[:, 1])
    b2 = b2.reshape(rows, cw)
    tm = PROJ_SAMPLE_ROWS
    assert rows % tm == 0 and tm % t == 0
    full = lambda width: pl.BlockSpec((tm, width), lambda i: (i, 0))
    outs = pl.pallas_call(
        functools.partial(_proj_kernel, seq_rows=t, d_model=d),
        grid=(rows // tm,),
        in_specs=[full(d), _resident((1, d)), _resident(w_in.shape), _resident(w_conv.shape),
                  full(cw), full(cw)],
        out_specs=[full(cw), full(cw), full(cw), full(cw), full(d), full(d), full(cw)],
        out_shape=[
            jax.ShapeDtypeStruct((rows, cw), BF16),
            jax.ShapeDtypeStruct((rows, cw), F32),
            jax.ShapeDtypeStruct((rows, cw), F32),
            jax.ShapeDtypeStruct((rows, cw), BF16),
            jax.ShapeDtypeStruct((rows, d), F32),
            jax.ShapeDtypeStruct((rows, d), F32),
            jax.ShapeDtypeStruct((rows, cw), F32),
        ],
        compiler_params=pltpu.CompilerParams(
            dimension_semantics=("arbitrary",), vmem_limit_bytes=VMEM_LIMIT),
        name="proj_sample",
    )(x.reshape(rows, d), g.reshape(1, d), w_in, w_conv, b1, b2)
    return outs


V_ROWS = HEAD_DIM + 16


def _select_blocks(scores, n_valid):
    blk_id = lax.broadcasted_iota(jnp.int32, scores.shape, 0)
    blk_f = blk_id.astype(F32)
    cand = jnp.where(blk_id < n_valid, scores, -jnp.inf)
    sel = jnp.zeros(scores.shape, jnp.bool_)
    for _ in range(MOBA_TOPK):
        mx = jnp.max(cand, axis=0, keepdims=True)
        idx = jnp.min(jnp.where(cand == mx, blk_f, float(scores.shape[0])), axis=0, keepdims=True)
        pick = (blk_f == idx) & (mx > -jnp.inf)
        sel = sel | pick
        cand = jnp.where(pick, -jnp.inf, cand)
    return sel


SEQ_PAIR = 2


def _attn_prompt_kernel(q_ref, qn_ref, k_ref, v_ref, o_ref, kaug_ref, vt_ref, kmean_ref, qaug_ref,
                        s_ref, *, n_blocks, chunk):
    t = pl.program_id(2)
    blk = MOBA_BLOCK
    heads = range(HEADS_PER_GROUP)
    A, B = range(SEQ_PAIR)

    def put_query(a, q_rows_ref, n_valid, slot):
        q_t = q_rows_ref[a].astype(F32).T
        d_row = lax.broadcasted_iota(jnp.int32, q_t.shape, 0)
        kmean = kmean_ref[a].astype(BF16)
        pad_rows = jnp.zeros((LANES - n_blocks, blk), BF16)
        for h in heads:
            in_head = (d_row >= h * HEAD_DIM) & (d_row < (h + 1) * HEAD_DIM)
            qh = jnp.where(in_head, q_t, 0.0).astype(BF16)
            pen = jnp.where(_select_blocks(_dot(kmean, qh), n_valid), 0.0, MASK_VALUE).astype(BF16)
            qaug_ref[a, slot, h] = jnp.concatenate([qh, pen, pad_rows], axis=0)

    @pl.when(t == 0)
    def _init():
        lane_k = lax.broadcasted_iota(jnp.int32, (blk, LANES), 1)
        pad_i = lax.broadcasted_iota(jnp.int32, (V_ROWS - HEAD_DIM, blk), 0)
        ones_rows = (pad_i == 0).astype(BF16)

        def fill(j, c):
            rows = pl.ds(pl.multiple_of(j * blk, blk), blk)
            for a in (A, B):
                kb = k_ref[a, rows, :]
                kaug_ref[a, rows, 0:LANES] = kb
                kaug_ref[a, rows, LANES:2 * LANES] = (lane_k == j).astype(BF16)
                v_t = v_ref[a, rows, :].astype(F32).T
                for h in heads:
                    vt_ref[a, j, h, 0:HEAD_DIM] = v_t[h * HEAD_DIM:(h + 1) * HEAD_DIM, :].astype(BF16)
                    vt_ref[a, j, h, HEAD_DIM:V_ROWS] = ones_rows
                kmean_ref[a, pl.ds(j, 1), :] = (
                    jnp.sum(kb.astype(F32), axis=0, keepdims=True) * (1.0 / blk))
            return c

        lax.fori_loop(0, n_blocks, fill, 0)
        for a in (A, B):
            put_query(a, q_ref, 0, 0)

    cur = t % 2
    q_aug = [[qaug_ref[a, cur, h] for h in heads] for a in (A, B)]
    key_i = lax.broadcasted_iota(jnp.int32, (blk, blk), 0)
    qry_i = lax.broadcasted_iota(jnp.int32, (blk, blk), 1)
    causal = key_i <= qry_i
    own = pl.ds(pl.multiple_of(t * blk, blk), blk)
    last_chunk = n_blocks // chunk - 1

    def put_scores(a, c, slot):
        rows = pl.ds(pl.multiple_of(c * (chunk * blk), chunk * blk), chunk * blk)
        kc = kaug_ref[a, rows, :]
        for h in heads:
            s_ref[a, slot, h] = _dot(kc, q_aug[a][h])

    def consume(a, c, slot, state):
        new = []
        for h in heads:
            m, acc = state[h]
            s = s_ref[a, slot, h]
            m_new = jnp.maximum(m, jnp.max(s, axis=0, keepdims=True))
            p = jnp.exp2(s - m_new).astype(BF16)
            acc = jnp.exp2(m - m_new) * acc
            for i in range(chunk):
                acc = acc + _dot(vt_ref[a, c * chunk + i, h], p[i * blk:(i + 1) * blk, :])
            new.append((m_new, acc))
        return tuple(new)

    own_scores = [[_dot(kaug_ref[a, own, 0:LANES], q_aug[a][h][0:LANES]) for h in heads]
                  for a in (A, B)]
    put_scores(A, 0, 0)
    for a in (A, B):
        put_query(a, qn_ref, t + 1, 1 - cur)
    state = []
    for a in (A, B):
        st = []
        for h in heads:
            s = jnp.where(causal, own_scores[a][h], MASK_VALUE)
            m = jnp.max(s, axis=0, keepdims=True)
            st.append((m, _dot(vt_ref[a, t, h], jnp.exp2(s - m).astype(BF16))))
        state.append(tuple(st))

    n_chunks = (t + chunk - 1) // chunk

    def step_pair(j, state):
        st_a, st_b = state
        put_scores(B, 2 * j, 0)
        st_a = consume(A, 2 * j, 0, st_a)
        put_scores(A, 2 * j + 1, 1)
        st_b = consume(B, 2 * j, 0, st_b)
        put_scores(B, 2 * j + 1, 1)
        st_a = consume(A, 2 * j + 1, 1, st_a)
        put_scores(A, jnp.minimum(2 * j + 2, last_chunk), 0)
        st_b = consume(B, 2 * j + 1, 1, st_b)
        return st_a, st_b

    def odd_tail(state):
        st_a, st_b = state
        put_scores(B, n_chunks - 1, 0)
        st_a = consume(A, n_chunks - 1, 0, st_a)
        return st_a, consume(B, n_chunks - 1, 0, st_b)

    state = lax.fori_loop(0, n_chunks // 2, step_pair, tuple(state))
    fin = lax.cond(n_chunks % 2 == 1, odd_tail, lambda st: st, state)
    for a in (A, B):
        o_t = jnp.concatenate(
            [acc[0:HEAD_DIM] * (1.0 / acc[HEAD_DIM:HEAD_DIM + 1]) for (_, acc) in fin[a]], axis=0)
        o_ref[a] = o_t.T.astype(o_ref.dtype)


def _attn_prompt(q, k, v):
    bsz, s, w = q.shape
    n_blocks = s // MOBA_BLOCK
    chunk = ATTN_CHUNK
    assert n_blocks % (2 * chunk) == 0 and n_blocks % 16 == 0 and n_blocks <= LANES
    assert bsz % SEQ_PAIR == 0
    groups = w // LANES
    q_blk = pl.BlockSpec((SEQ_PAIR, MOBA_BLOCK, LANES), lambda b, g, t: (b, t, g))
    q_next = pl.BlockSpec((SEQ_PAIR, MOBA_BLOCK, LANES),
                          lambda b, g, t: (b, jnp.minimum(t + 1, n_blocks - 1), g))
    kv_blk = pl.BlockSpec((SEQ_PAIR, s, LANES), lambda b, g, t: (b, 0, g))
    return pl.pallas_call(
        functools.partial(_attn_prompt_kernel, n_blocks=n_blocks, chunk=chunk),
        grid=(bsz // SEQ_PAIR, groups, n_blocks),
        in_specs=[q_blk, q_next, kv_blk, kv_blk],
        out_specs=q_blk,
        out_shape=jax.ShapeDtypeStruct((bsz, s, w), BF16),
        scratch_shapes=[
            pltpu.VMEM((SEQ_PAIR, s, 2 * LANES), BF16),
            pltpu.VMEM((SEQ_PAIR, n_blocks, HEADS_PER_GROUP, V_ROWS, MOBA_BLOCK), BF16),
            pltpu.VMEM((SEQ_PAIR, n_blocks, LANES), F32),
            pltpu.VMEM((SEQ_PAIR, 2, HEADS_PER_GROUP, 2 * LANES, MOBA_BLOCK), BF16),
            pltpu.VMEM((SEQ_PAIR, 2, HEADS_PER_GROUP, chunk * MOBA_BLOCK, MOBA_BLOCK), F32),
        ],
        compiler_params=pltpu.CompilerParams(
            dimension_semantics=("arbitrary", "arbitrary", "arbitrary"),
            vmem_limit_bytes=VMEM_LIMIT),
        name="attn_prompt",
    )(q, q, k, v)


def _attn_sample_kernel(pt_ref, q_ref, kn_ref, vn_ref, ck_hbm, cv_hbm, o_ref, kbuf, vbuf, sem, *,
                        n_pages, n_new, n_seq):
    this_seq = pl.program_id(0)
    pages_per_blk = MOBA_BLOCK // PAGE_SIZE
    n_blk = n_pages // pages_per_blk

    def page_copies(seq):
        slot = seq % PAGE_SLOTS
        copies = []
        for p in range(n_pages):
            page = pt_ref[seq * n_pages + p]
            copies.append(pltpu.make_async_copy(ck_hbm.at[page], kbuf.at[slot, p], sem.at[0, slot, p]))
            copies.append(pltpu.make_async_copy(cv_hbm.at[page], vbuf.at[slot, p], sem.at[1, slot, p]))
        return copies

    @pl.when(this_seq == 0)
    def _():
        for ahead in range(min(PAGE_SLOTS - 1, n_seq)):
            for c in page_copies(ahead):
                c.start()

    @pl.when(this_seq + PAGE_SLOTS - 1 < n_seq)
    def _():
        for c in page_copies(this_seq + PAGE_SLOTS - 1):
            c.start()

    for c in page_copies(this_seq):
        c.wait()
    slot = this_seq % PAGE_SLOTS
    k_refs = [kbuf.at[slot, p] for p in range(n_pages)]
    v_refs = [vbuf.at[slot, p] for p in range(n_pages)]

    q = q_ref[...]
    r = lax.broadcasted_iota(jnp.int32, q.shape, 0)
    lane = lax.broadcasted_iota(jnp.int32, q.shape, 1)
    diag = (lane // HEAD_DIM) == (r // n_new)
    qbd = jnp.where(diag, q, jnp.zeros_like(q))
    qf = qbd.astype(F32)

    def block_of(refs, j):
        pages = [refs[j * pages_per_blk + i][...].astype(BF16) for i in range(pages_per_blk)]
        return jnp.concatenate(pages, axis=1)

    s_blk = [_dot(qbd, block_of(k_refs, j)) for j in range(n_blk)]
    sb = [jnp.sum(s, axis=-1, keepdims=True) * (1.0 / MOBA_BLOCK) for s in s_blk]

    sel = []
    for j in range(n_blk):
        rank = jnp.zeros(sb[j].shape, jnp.int32)
        for i in range(n_blk):
            if i == j:
                continue
            beats = (sb[i] >= sb[j]) if i < j else (sb[i] > sb[j])
            rank = rank + beats.astype(jnp.int32)
        sel.append(rank < MOBA_TOPK)

    tok = r[:, 0:1] % n_new
    s_past = [jnp.where(sel[j], s_blk[j], MASK_VALUE) for j in range(n_blk)]
    def own_rows(ref):
        rows8 = ref[...]
        out = rows8[0:n_new]
        for c in range(1, 8 // n_new):
            out = jnp.where(this_seq % (8 // n_new) == c, rows8[c * n_new:(c + 1) * n_new], out)
        return out

    kn = own_rows(kn_ref)
    vn = own_rows(vn_ref)
    s_new = [jnp.where(tok >= i, jnp.sum(qf * kn[i:i + 1, :], axis=-1, keepdims=True), MASK_VALUE)
             for i in range(n_new)]
    m = s_new[0]
    for s in s_new[1:]:
        m = jnp.maximum(m, s)
    for s in s_past:
        m = jnp.maximum(m, jnp.max(s, axis=-1, keepdims=True))

    l = jnp.zeros(m.shape, F32)
    acc = jnp.zeros(q.shape, F32)
    for i in range(n_new):
        p = jnp.exp2(s_new[i] - m)
        l = l + p
        acc = acc + p * vn[i:i + 1, :]
    for j in range(n_blk):
        p = jnp.exp2(s_past[j] - m)
        l = l + jnp.sum(p, axis=-1, keepdims=True)
        acc = acc + _dot_nt(p.astype(BF16), block_of(v_refs, j))
    o = jnp.where(diag, acc / l, 0.0)
    while o.shape[0] > 8:
        half = o.shape[0] // 2
        o = o[:half] + o[half:]
    shift = 4
    while shift >= n_new:
        o = o + pltpu.roll(o, shift, axis=0)
        shift //= 2
    o_ref[...] = o.astype(o_ref.dtype)


def _attn_sample(q, k_new, v_new, cache_k, cache_v, page_table):
    n, t, w = q.shape
    n_pages = page_table.shape[1]
    assert (n_pages * PAGE_SIZE) % MOBA_BLOCK == 0 and MOBA_BLOCK % PAGE_SIZE == 0
    assert t in (1, 2, 4, 8) and n_pages * PAGE_SIZE // MOBA_BLOCK >= MOBA_TOPK
    assert k_new.shape == (n * t, w) and (n * t) % 8 == 0
    n_pool = cache_k.shape[0]
    ck = cache_k.transpose(0, 2, 3, 1).reshape(n_pool, w, PAGE_SIZE)
    cv = cache_v.transpose(0, 2, 3, 1).reshape(n_pool, w, PAGE_SIZE)
    q_rep = jnp.tile(q, (1, N_HEADS, 1))

    grid_spec = pltpu.PrefetchScalarGridSpec(
        num_scalar_prefetch=1,
        grid=(n,),
        in_specs=[
            pl.BlockSpec((None, N_HEADS * t, w), lambda i, pt: (i, 0, 0)),
            pl.BlockSpec((8, w), lambda i, pt: (i // (8 // t), 0)),
            pl.BlockSpec((8, w), lambda i, pt: (i // (8 // t), 0)),
            pl.BlockSpec(memory_space=pl.ANY),
            pl.BlockSpec(memory_space=pl.ANY),
        ],
        out_specs=pl.BlockSpec((None, 8, w), lambda i, pt: (i, 0, 0)),
        scratch_shapes=[
            pltpu.VMEM((PAGE_SLOTS, n_pages, w, PAGE_SIZE), F32),
            pltpu.VMEM((PAGE_SLOTS, n_pages, w, PAGE_SIZE), F32),
            pltpu.SemaphoreType.DMA((2, PAGE_SLOTS, n_pages)),
        ],
    )
    out = pl.pallas_call(
        functools.partial(_attn_sample_kernel, n_pages=n_pages, n_new=t, n_seq=n),
        grid_spec=grid_spec,
        out_shape=jax.ShapeDtypeStruct((n, 8, w), BF16),
        compiler_params=pltpu.CompilerParams(
            dimension_semantics=("arbitrary",), vmem_limit_bytes=VMEM_LIMIT),
        name="attn_sample",
    )(page_table.reshape(-1), q_rep, k_new, v_new, ck, cv)
    return out[:, :t]


def _merge_ffn_kernel(x_ref, ao_ref, co_ref, ga_ref, gc_ref, wpa_ref, wpc_ref, wo_ref,
                      g1_ref, g2_ref, g3_ref, wg_ref, wu_ref, wd_ref, y_ref, a_ref, *, ff_chunk):
    m = ga_ref[...] * _dot(ao_ref[...], wpa_ref[...]) + gc_ref[...] * _dot(co_ref[...], wpc_ref[...])
    h = x_ref[...] + _rms(_dot(m.astype(BF16), wo_ref[...])) * g1_ref[...]
    hn = (_rms(h) * g2_ref[...]).astype(BF16)
    d_ff = wg_ref.shape[1]
    for c in range(0, d_ff, ff_chunk):
        g = _dot(hn, wg_ref[:, c:c + ff_chunk])
        u = _dot(hn, wu_ref[:, c:c + ff_chunk])
        a_ref[:, c:c + ff_chunk] = (g * jax.nn.sigmoid(g) * u).astype(BF16)
    f = _dot(a_ref[...], wd_ref[...])
    y_ref[...] = h + _rms(f) * g3_ref[...]


def _merge_ffn(x, ao, co, ga, gc, wpa, wpc, wo, g1, g2, g3, wg, wu, wd, tm):
    rows, d = x.shape
    cw = ao.shape[1]
    d_ff = wg.shape[1]
    ff_chunk = FF_CHUNK
    assert d_ff % ff_chunk == 0 and rows % tm == 0
    row_blk = lambda width: pl.BlockSpec((tm, width), lambda i: (i, 0))
    return pl.pallas_call(
        functools.partial(_merge_ffn_kernel, ff_chunk=ff_chunk),
        grid=(rows // tm,),
        in_specs=[row_blk(d), row_blk(cw), row_blk(cw), row_blk(d), row_blk(d),
                  _resident(wpa.shape), _resident(wpc.shape), _resident(wo.shape),
                  _resident((1, d)), _resident((1, d)), _resident((1, d)),
                  _resident(wg.shape), _resident(wu.shape), _resident(wd.shape)],
        out_specs=row_blk(d),
        out_shape=jax.ShapeDtypeStruct((rows, d), F32),
        scratch_shapes=[pltpu.VMEM((tm, d_ff), BF16)],
        compiler_params=pltpu.CompilerParams(
            dimension_semantics=("arbitrary",), vmem_limit_bytes=VMEM_LIMIT),
        name="merge_ffn",
    )(x, ao, co, ga, gc, wpa, wpc, wo, g1.reshape(1, d), g2.reshape(1, d), g3.reshape(1, d),
      wg, wu, wd)


def kernel(x_prompt, x_sample, cache_k, cache_v, state_conv, page_table, g_attn_pre, w_in, w_conv, w_proj_attn, w_proj_conv, w_out, g_attn_post, g_ffn_pre, w_gate, w_up, w_down, g_ffn_post):
    depth = w_in.shape[0]
    bsz, s, d = x_prompt.shape
    n, t, _ = x_sample.shape
    cw = ATTN_WIDTH
    assert t >= CONV_WIDTH - 1 and s >= CONV_WIDTH - 1
    xp, xs = x_prompt, x_sample
    outs = [[] for _ in range(6)]
    for l in range(depth):
        w_in_b = w_in[l].astype(BF16)
        tail = (w_proj_attn[l].astype(BF16), w_proj_conv[l].astype(BF16), w_out[l].astype(BF16),
                g_attn_post[l], g_ffn_pre[l], g_ffn_post[l],
                w_gate[l].astype(BF16), w_up[l].astype(BF16), w_down[l].astype(BF16))

        prev0 = jnp.zeros((bsz, CONV_WIDTH - 1, cw), F32)
        q, k, v, kb, vb, co, ga, gc, u_tail = _proj_prompt(xp, g_attn_pre[l], w_in_b, w_conv[l],
                                                              prev0, PROJ_ROWS)
        ao = _attn_prompt(q, kb, vb)
        flat = lambda a: a.reshape(bsz * s, a.shape[-1])
        xp = _merge_ffn(flat(xp), flat(ao), flat(co), flat(ga), flat(gc), *tail, MERGE_ROWS)
        xp = xp.reshape(bsz, s, d)
        outs[0].append(k.reshape(bsz, s, N_HEADS, HEAD_DIM))
        outs[1].append(v.reshape(bsz, s, N_HEADS, HEAD_DIM))
        outs[2].append(u_tail[:, 8 - (CONV_WIDTH - 1):])

        q, k, v, co, ga, gc, u = _proj_sample(xs, g_attn_pre[l], w_in_b, w_conv[l], state_conv[l])
        ao = _attn_sample(q.reshape(n, t, cw), k, v, cache_k[l], cache_v[l], page_table)
        xs = _merge_ffn(xs.reshape(n * t, d), ao.reshape(n * t, cw), co, ga, gc, *tail,
                        MERGE_SAMPLE_ROWS)
        xs = xs.reshape(n, t, d)
        outs[3].append(k.reshape(n, t, N_HEADS, HEAD_DIM))
        outs[4].append(v.reshape(n, t, N_HEADS, HEAD_DIM))
        outs[5].append(u.reshape(n, t, cw)[:, t - (CONV_WIDTH - 1):])

    kp, vp, cp, ks, vs, cs = (jnp.stack(o) for o in outs)
    return (xp, xs, kp, vp, cp, ks, vs, cs)
```
